```python
import math
import jax, jax.numpy as jnp
from jax import lax
import numpy as np

D_MODEL = 1024
BATCH = 32
SEQ = 2048
DEPTH = 1
DEC_BATCH = 16
DEC_SEQ = 2048
PAST_LEN = 128

GRID_W = 64
HEAD_DIM = 64
N_Q_HEADS = 8
N_KV_HEADS = 2
N_GROUPS_PER_KV = N_Q_HEADS // N_KV_HEADS
D_ATTN = N_Q_HEADS * HEAD_DIM
D_KV = N_KV_HEADS * HEAD_DIM
AXIS_DIM = HEAD_DIM // 2
HALF_AXIS = AXIS_DIM // 2
ROPE_THETA = 10000.0
Q_BLOCK = 128
D_CONV = D_MODEL - D_ATTN
CONV_K = 31
CONV_PAD = (CONV_K - 1) // 2
D_IN_PROJ = D_ATTN + 2 * D_KV + 2 * D_CONV
N_EXPERTS = 32
TOP_K = 4
D_FF = D_MODEL
SWIGLU_ALPHA = 1.702
SWIGLU_LIMIT = 7.0
MOE_BLOCK = 256
EPS = 1e-6

kernel_name = "hymba_conformer_gqa_axialrope_moe_encoder"


def _rmsnorm(x, g):
    xf = x.astype(jnp.float32)
    y = xf * lax.rsqrt(jnp.mean(xf * xf, axis=-1, keepdims=True) + EPS)
    return (y * g.astype(jnp.float32)).astype(x.dtype)


def _layernorm(x, g, b):
    xf = x.astype(jnp.float32)
    mu = jnp.mean(xf, axis=-1, keepdims=True)
    xc = xf - mu
    var = jnp.mean(xc * xc, axis=-1, keepdims=True)
    y = xc * lax.rsqrt(var + EPS) * g.astype(jnp.float32) + b.astype(jnp.float32)
    return y.astype(x.dtype)


def _axial_rope_tables(seq_len):
    rows = seq_len // GRID_W
    row = jnp.repeat(jnp.arange(rows), GRID_W).astype(jnp.float32)
    col = jnp.tile(jnp.arange(GRID_W), rows).astype(jnp.float32)
    inv = 1.0 / (ROPE_THETA ** (jnp.arange(0, AXIS_DIM, 2, dtype=jnp.float32) / AXIS_DIM))
    ang_r = row[:, None] * inv[None, :]
    ang_c = col[:, None] * inv[None, :]
    ang = jnp.concatenate([ang_r, ang_r, ang_c, ang_c], axis=-1)
    return jnp.cos(ang), jnp.sin(ang)


def _apply_axial_rope(x, cos, sin):
    xa = x.reshape(*x.shape[:-1], 2, 2, HALF_AXIS)
    rot = jnp.stack([-xa[..., 1, :], xa[..., 0, :]], axis=-2).reshape(x.shape)
    return x * cos[None, :, None, :] + rot * sin[None, :, None, :]


def _block_attention(q, k, v):
    B, S = q.shape[0], q.shape[1]
    nb = S // Q_BLOCK
    qb = q.reshape(B, nb, Q_BLOCK, N_KV_HEADS, N_GROUPS_PER_KV, HEAD_DIM).transpose(1, 0, 2, 3, 4, 5)

    def one_block(qi):
        s = jnp.einsum('bqkgd,bskd->bkgqs', qi, k).astype(jnp.float32)
        p = jax.nn.softmax(s, axis=-1).astype(v.dtype)
        return jnp.einsum('bkgqs,bskd->bqkgd', p, v)

    o = lax.map(one_block, qb)
    return o.transpose(1, 0, 2, 3, 4, 5).reshape(B, S, D_ATTN)


def _conformer_conv(u2, dw_w, dw_b, ln_g, ln_b, w_pw2):
    a, gte = jnp.split(u2, 2, axis=-1)
    u = a * jax.nn.sigmoid(gte)
    u = lax.conv_general_dilated(u, dw_w[:, None, :].astype(u.dtype), window_strides=(1,),
                                 padding=[(CONV_PAD, CONV_PAD)],
                                 dimension_numbers=('NWC', 'WIO', 'NWC'),
                                 feature_group_count=D_CONV) + dw_b.astype(u.dtype)
    u = _layernorm(u, ln_g, ln_b)
    u = jax.nn.silu(u)
    return u @ w_pw2


def _token_mixer(h, w_in, q_norm_g, k_norm_g, dw_w, dw_b, ln_g, ln_b, w_pw2, attn_out_g, conv_out_g, w_out):
    B, S, _ = h.shape
    z = h @ w_in
    q = z[..., :D_ATTN].reshape(B, S, N_Q_HEADS, HEAD_DIM)
    k = z[..., D_ATTN:D_ATTN + D_KV].reshape(B, S, N_KV_HEADS, HEAD_DIM)
    v = z[..., D_ATTN + D_KV:D_ATTN + 2 * D_KV].reshape(B, S, N_KV_HEADS, HEAD_DIM)
    u2 = z[..., D_ATTN + 2 * D_KV:]
    cos, sin = _axial_rope_tables(S)
    qf = _apply_axial_rope(_rmsnorm(q, q_norm_g).astype(jnp.float32), cos, sin) * (1.0 / math.sqrt(HEAD_DIM))
    kf = _apply_axial_rope(_rmsnorm(k, k_norm_g).astype(jnp.float32), cos, sin)
    attn = _block_attention(qf.astype(h.dtype), kf.astype(h.dtype), v)
    conv = _conformer_conv(u2, dw_w, dw_b, ln_g, ln_b, w_pw2)
    merged = jnp.concatenate([_rmsnorm(conv, conv_out_g), _rmsnorm(attn, attn_out_g)], axis=-1)
    return merged @ w_out


def _clamped_swiglu(hgu):
    x_glu = jnp.minimum(hgu[..., ::2], SWIGLU_LIMIT)
    x_lin = jnp.clip(hgu[..., 1::2], -SWIGLU_LIMIT, SWIGLU_LIMIT)
    return x_glu * jax.nn.sigmoid(SWIGLU_ALPHA * x_glu) * (x_lin + 1.0)


def _moe(h, router_w, router_b, w_gu, b_gu, w_dn, b_dn):
    B, S, D = h.shape
    T = B * S
    xf = h.reshape(T, D)
    logits = xf.astype(jnp.float32) @ router_w.astype(jnp.float32) + router_b.astype(jnp.float32)
    top_v, top_i = lax.top_k(logits, TOP_K)
    gates = jax.nn.softmax(top_v, axis=-1)
    M = T * TOP_K
    e_flat = top_i.reshape(M).astype(jnp.int32)
    tok_flat = jnp.arange(M, dtype=jnp.int32) // TOP_K
    order = jnp.argsort(e_flat)
    e_sorted = e_flat[order]
    tok_sorted = tok_flat[order]
    g_sorted = gates.reshape(M)[order]
    counts = jnp.bincount(e_flat, length=N_EXPERTS).astype(jnp.int32)
    padded = ((counts + MOE_BLOCK - 1) // MOE_BLOCK) * MOE_BLOCK
    pend = jnp.cumsum(padded)
    pstart = pend - padded
    start = jnp.cumsum(counts) - counts
    dest = pstart[e_sorted] + (jnp.arange(M, dtype=jnp.int32) - start[e_sorted])
    n_blocks = M // MOE_BLOCK + N_EXPERTS
    P = n_blocks * MOE_BLOCK
    slot_tok = jnp.full((P,), T, dtype=jnp.int32).at[dest].set(tok_sorted)
    slot_gate = jnp.zeros((P,), jnp.float32).at[dest].set(g_sorted)
    block_start = jnp.arange(n_blocks, dtype=jnp.int32) * MOE_BLOCK
    block_expert = jnp.minimum(jnp.searchsorted(pend, block_start, side='right'), N_EXPERTS - 1).astype(jnp.int32)
    x_pad = jnp.concatenate([xf, jnp.zeros((1, D), xf.dtype)], axis=0)
    xs = x_pad[slot_tok].reshape(n_blocks, MOE_BLOCK, D)

    def expert_block(args):
        xb, e = args
        hgu = xb @ w_gu[e] + b_gu[e]
        return _clamped_swiglu(hgu) @ w_dn[e] + b_dn[e]

    ys = lax.map(expert_block, (xs, block_expert)).reshape(P, D)
    ys = ys * slot_gate[:, None].astype(ys.dtype)
    out = jnp.zeros((T + 1, D), ys.dtype).at[slot_tok].add(ys)[:T]
    return out.reshape(B, S, D)


def _trunk(x, norm1_g, w_in, q_norm_g, k_norm_g, conv_dw_w, conv_dw_b, conv_ln_g, conv_ln_b, w_pw2,
           attn_out_g, conv_out_g, w_out, norm2_g, router_w, router_b, w_gate_up, b_gate_up,
           w_down, b_down, final_g):
    for l in range(DEPTH):
        h = _rmsnorm(x, norm1_g[l])
        x = x + _token_mixer(h, w_in[l], q_norm_g[l], k_norm_g[l], conv_dw_w[l], conv_dw_b[l],
                             conv_ln_g[l], conv_ln_b[l], w_pw2[l], attn_out_g[l], conv_out_g[l], w_out[l])
        h = _rmsnorm(x, norm2_g[l])
        x = x + _moe(h, router_w[l], router_b[l], w_gate_up[l], b_gate_up[l], w_down[l], b_down[l])
    return _rmsnorm(x, final_g)


def setup_inputs(seed: int = 0) -> dict:
    key = jax.random.key(seed)
    ks = jax.random.split(key, 24)
    f = jnp.float32
    def nrm(k, shape, scale):
        return jax.random.normal(k, shape, f) * scale
    return {
        "x_prompt": nrm(ks[0], (BATCH, SEQ, D_MODEL), 1.0),
        "x_sample": nrm(ks[1], (DEC_BATCH, DEC_SEQ, D_MODEL), 1.0),
        "norm1_g": 1.0 + nrm(ks[2], (DEPTH, D_MODEL), 0.02),
        "w_in": nrm(ks[3], (DEPTH, D_MODEL, D_IN_PROJ), D_MODEL ** -0.5),
        "q_norm_g": 1.0 + nrm(ks[4], (DEPTH, HEAD_DIM), 0.02),
        "k_norm_g": 1.0 + nrm(ks[5], (DEPTH, HEAD_DIM), 0.02),
        "conv_dw_w": nrm(ks[6], (DEPTH, CONV_K, D_CONV), CONV_K ** -0.5),
        "conv_dw_b": nrm(ks[7], (DEPTH, D_CONV), 0.02),
        "conv_ln_g": 1.0 + nrm(ks[8], (DEPTH, D_CONV), 0.02),
        "conv_ln_b": nrm(ks[9], (DEPTH, D_CONV), 0.02),
        "w_pw2": nrm(ks[10], (DEPTH, D_CONV, D_CONV), D_CONV ** -0.5),
        "attn_out_g": 1.0 + nrm(ks[11], (DEPTH, D_ATTN), 0.02),
        "conv_out_g": 1.0 + nrm(ks[12], (DEPTH, D_CONV), 0.02),
        "w_out": nrm(ks[13], (DEPTH, D_MODEL, D_MODEL), D_MODEL ** -0.5),
        "norm2_g": 1.0 + nrm(ks[14], (DEPTH, D_MODEL), 0.02),
        "router_w": nrm(ks[15], (DEPTH, D_MODEL, N_EXPERTS), D_MODEL ** -0.5),
        "router_b": nrm(ks[16], (DEPTH, N_EXPERTS), 0.01),
        "w_gate_up": nrm(ks[17], (DEPTH, N_EXPERTS, D_MODEL, 2 * D_FF), D_MODEL ** -0.5),
        "b_gate_up": nrm(ks[18], (DEPTH, N_EXPERTS, 2 * D_FF), 0.02),
        "w_down": nrm(ks[19], (DEPTH, N_EXPERTS, D_FF, D_MODEL), D_FF ** -0.5),
        "b_down": nrm(ks[20], (DEPTH, N_EXPERTS, D_MODEL), 0.02),
        "final_g": 1.0 + nrm(ks[21], (D_MODEL,), 0.02),
    }


def reference(x_prompt, x_sample, norm1_g, w_in, q_norm_g, k_norm_g, conv_dw_w, conv_dw_b, conv_ln_g,
              conv_ln_b, w_pw2, attn_out_g, conv_out_g, w_out, norm2_g, router_w, router_b, w_gate_up,
              b_gate_up, w_down, b_down, final_g):
    y_prompt = _trunk(x_prompt, norm1_g, w_in, q_norm_g, k_norm_g, conv_dw_w, conv_dw_b, conv_ln_g, conv_ln_b,
                      w_pw2, attn_out_g, conv_out_g, w_out, norm2_g, router_w, router_b, w_gate_up, b_gate_up,
                      w_down, b_down, final_g)
    y_sample = _trunk(x_sample, norm1_g, w_in, q_norm_g, k_norm_g, conv_dw_w, conv_dw_b, conv_ln_g, conv_ln_b,
                      w_pw2, attn_out_g, conv_out_g, w_out, norm2_g, router_w, router_b, w_gate_up, b_gate_up,
                      w_down, b_down, final_g)
    return (y_prompt, y_sample)
```

```python
import functools
import math

import numpy as np
import jax
import jax.numpy as jnp
from jax import lax
from jax.experimental import pallas as pl
from jax.experimental.pallas import tpu as pltpu

F32 = jnp.float32
BF16 = jnp.bfloat16
I32 = jnp.int32

D_MODEL = 1024
GRID_W = 64
HEAD_DIM = 64
N_Q_HEADS = 8
N_KV_HEADS = 2
N_GROUPS_PER_KV = N_Q_HEADS // N_KV_HEADS
D_ATTN = N_Q_HEADS * HEAD_DIM
D_KV = N_KV_HEADS * HEAD_DIM
D_QK = D_ATTN + D_KV
AXIS_DIM = HEAD_DIM // 2
HALF_AXIS = AXIS_DIM // 2
ROPE_THETA = 10000.0
D_CONV = D_MODEL - D_ATTN
CONV_K = 31
CONV_PAD = (CONV_K - 1) // 2
D_IN_PROJ = D_ATTN + 2 * D_KV + 2 * D_CONV
N_EXPERTS = 32
TOP_K = 4
D_FF = D_MODEL
SWIGLU_ALPHA = 1.702
SWIGLU_LIMIT = 7.0
EPS = 1e-6

LANES = 128
HALO = 16
NEG_INF = float("-inf")

PRE_ROWS = 512
ATTN_ROWS = 256
POST_ROWS = 256
TOKEN_ROWS = 256
EXPERT_ROWS = 256

VMEM_LIMIT = 56 * 1024 * 1024


def _cparams(sem):
    return pltpu.CompilerParams(dimension_semantics=sem, vmem_limit_bytes=VMEM_LIMIT)


def _rms(x, g):
    return x * lax.rsqrt(jnp.mean(x * x, axis=-1, keepdims=True) + EPS) * g


def _split_bf16(x):
    hi = x.astype(BF16)
    lo = (x - hi.astype(F32)).astype(BF16)
    return hi, lo


def _pre_body(x_ref, g1_ref, win_ref, qkg_ref, cos_ref, sin_ref, hsum_ref,
              q_ref, k_ref, v_ref, u_ref):
    h = _rms(x_ref[...], g1_ref[...]).astype(BF16)
    z = jnp.dot(h, win_ref[...], preferred_element_type=F32)
    cos = cos_ref[...]
    sin = sin_ref[...]
    lane = lax.broadcasted_iota(I32, (1, LANES), 1)
    first_half = (lane % AXIS_DIM) < HALF_AXIS
    hsum = hsum_ref[...]
    for c in range(D_QK // LANES):
        zc = z[:, c * LANES:(c + 1) * LANES]
        hi, lo = _split_bf16(zc * zc)
        ss = (jnp.dot(hi, hsum, preferred_element_type=F32)
              + jnp.dot(lo, hsum, preferred_element_type=F32))
        zn = zc * lax.rsqrt(ss * (1.0 / HEAD_DIM) + EPS) * qkg_ref[:, c * LANES:(c + 1) * LANES]
        rot = jnp.where(first_half,
                        pltpu.roll(zn, LANES - HALF_AXIS, 1),
                        pltpu.roll(zn, HALF_AXIS, 1))
        out = (zn * cos + rot * sin).astype(BF16)
        if c < D_ATTN // LANES:
            q_ref[:, c * LANES:(c + 1) * LANES] = out
        else:
            k_ref[...] = out
    v_ref[...] = z[:, D_QK:D_QK + D_KV].astype(BF16)
    a = z[:, D_QK + D_KV:D_QK + D_KV + D_CONV]
    gate = z[:, D_QK + D_KV + D_CONV:]
    u_ref[...] = a * jax.nn.sigmoid(gate)


def _pre(x, g1, w_in, qk_gain, cos, sin, hsum):
    B, S, _ = x.shape
    rows = min(PRE_ROWS, S)
    grid = (B, S // rows)
    row_blk = lambda w: pl.BlockSpec((None, rows, w), lambda b, i: (b, i, 0))
    full = lambda a: pl.BlockSpec(a.shape, lambda b, i: (0,) * a.ndim)
    return pl.pallas_call(
        _pre_body,
        grid=grid,
        in_specs=[row_blk(D_MODEL), full(g1), full(w_in), full(qk_gain),
                  pl.BlockSpec((rows, LANES), lambda b, i: (i, 0)),
                  pl.BlockSpec((rows, LANES), lambda b, i: (i, 0)),
                  full(hsum)],
        out_specs=[row_blk(D_ATTN), row_blk(D_KV), row_blk(D_KV), row_blk(D_CONV)],
        out_shape=[jax.ShapeDtypeStruct((B, S, D_ATTN), BF16),
                   jax.ShapeDtypeStruct((B, S, D_KV), BF16),
                   jax.ShapeDtypeStruct((B, S, D_KV), BF16),
                   jax.ShapeDtypeStruct((B, S, D_CONV), F32)],
        compiler_params=_cparams(("arbitrary", "arbitrary")),
        name="pre",
    )(x, g1, w_in, qk_gain, cos, sin, hsum)


def _attn_body(q_ref, k_ref, v_ref, o_ref):
    nt = (((1,), (1,)), ((), ()))
    for kv in range(N_KV_HEADS):
        kh = k_ref[:, kv * HEAD_DIM:(kv + 1) * HEAD_DIM]
        vh = v_ref[:, kv * HEAD_DIM:(kv + 1) * HEAD_DIM]
        for g in range(N_GROUPS_PER_KV):
            h = kv * N_GROUPS_PER_KV + g
            qh = q_ref[:, h * HEAD_DIM:(h + 1) * HEAD_DIM]
            s = lax.dot_general(qh, kh, nt, preferred_element_type=F32)
            m = jnp.max(s, axis=-1, keepdims=True)
            p = jnp.exp(s - m)
            l = jnp.sum(p, axis=-1, keepdims=True)
            o = jnp.dot(p.astype(BF16), vh, preferred_element_type=F32)
            o_ref[:, h * HEAD_DIM:(h + 1) * HEAD_DIM] = (o / l).astype(BF16)


def _attention(q, k, v):
    B, S, _ = q.shape
    rows = min(ATTN_ROWS, S)
    return pl.pallas_call(
        _attn_body,
        grid=(B, S // rows),
        in_specs=[pl.BlockSpec((None, rows, D_ATTN), lambda b, i: (b, i, 0)),
                  pl.BlockSpec((None, S, D_KV), lambda b, i: (b, 0, 0)),
                  pl.BlockSpec((None, S, D_KV), lambda b, i: (b, 0, 0))],
        out_specs=pl.BlockSpec((None, rows, D_ATTN), lambda b, i: (b, i, 0)),
        out_shape=jax.ShapeDtypeStruct((B, S, D_ATTN), BF16),
        compiler_params=_cparams(("arbitrary", "arbitrary")),
        name="attn",
    )(q, k, v)


def _top4(lg, axis, iota):
    vals, idxs, sels = [], [], []
    for _ in range(TOP_K):
        m = jnp.max(lg, axis=axis, keepdims=True)
        idx = jnp.min(jnp.where(lg == m, iota, N_EXPERTS * 4), axis=axis, keepdims=True)
        sel = iota == idx
        vals.append(m)
        idxs.append(idx)
        sels.append(sel)
        lg = jnp.where(sel, NEG_INF, lg)
    return vals, idxs, sels


def _post_body(up_ref, um_ref, un_ref, attn_ref, x_ref,
               dww_ref, dwb_ref, lng_ref, lnb_ref, wpw_ref, cog_ref, aog_ref, wout_ref,
               g2_ref, rwh_ref, rwl_ref, rb_ref, tri_ref,
               x1_ref, h2_ref, idx_ref, rank_ref, gate_ref, cnt_ref,
               wpad_ref, conv_ref, run_ref):
    b = pl.program_id(0)
    i = pl.program_id(1)
    n_i = pl.num_programs(1)
    rows = um_ref.shape[0]

    @pl.when((b == 0) & (i == 0))
    def _():
        run_ref[...] = jnp.zeros_like(run_ref)

    wpad_ref[0:HALO, :] = jnp.where(i > 0, up_ref[...], 0.0)
    wpad_ref[HALO:HALO + rows, :] = um_ref[...]
    wpad_ref[HALO + rows:, :] = jnp.where(i < n_i - 1, un_ref[...], 0.0)
    for c in range(D_CONV // LANES):
        cs = slice(c * LANES, (c + 1) * LANES)
        acc = jnp.zeros((rows, LANES), F32)
        for j in range(CONV_K):
            off = HALO - CONV_PAD + j
            acc = acc + wpad_ref[off:off + rows, cs] * dww_ref[j:j + 1, cs]
        conv_ref[:, cs] = acc + dwb_ref[:, cs]
    y = conv_ref[...]
    mu = jnp.mean(y, axis=-1, keepdims=True)
    yc = y - mu
    var = jnp.mean(yc * yc, axis=-1, keepdims=True)
    y = yc * lax.rsqrt(var + EPS) * lng_ref[...] + lnb_ref[...]
    y = y * jax.nn.sigmoid(y)
    conv_out = jnp.dot(y.astype(BF16), wpw_ref[...], preferred_element_type=F32)

    cn = _rms(conv_out, cog_ref[...]).astype(BF16)
    an = _rms(attn_ref[...].astype(F32), aog_ref[...]).astype(BF16)
    mix = (jnp.dot(cn, wout_ref[0:D_CONV, :], preferred_element_type=F32)
           + jnp.dot(an, wout_ref[D_CONV:, :], preferred_element_type=F32))
    x1 = x_ref[...] + mix
    x1_ref[...] = x1
    h2 = _rms(x1, g2_ref[...])
    h2_ref[...] = h2

    hi, lo = _split_bf16(h2)
    rwh = rwh_ref[...]
    lg = (jnp.dot(hi, rwh, preferred_element_type=F32)
          + jnp.dot(lo, rwh, preferred_element_type=F32)
          + jnp.dot(hi, rwl_ref[...], preferred_element_type=F32)) + rb_ref[...]
    lane = lax.broadcasted_iota(I32, (rows, LANES), 1)
    lg = jnp.where(lane < N_EXPERTS, lg, NEG_INF)

    vals, _, _ = _top4(lg, 1, lane)
    es = [jnp.exp(v - vals[0]) for v in vals]
    inv = 1.0 / (es[0] + es[1] + es[2] + es[3])
    gl = lax.broadcasted_iota(I32, (rows, gate_ref.shape[1]), 1)
    gates = jnp.zeros((rows, gate_ref.shape[1]), F32)
    for kk in range(TOP_K):
        gates = jnp.where(gl == kk, es[kk] * inv, gates)
    gate_ref[...] = gates

    lgt = lg.T[0:N_EXPERTS, :]
    eio = lax.broadcasted_iota(I32, (N_EXPERTS, rows), 0)
    _, idxs, sels = _top4(lgt, 0, eio)
    chosen = sels[0] | sels[1] | sels[2] | sels[3]
    onehot = jnp.where(chosen, 1.0, 0.0)
    prefix = jnp.dot(onehot.astype(BF16), tri_ref[...], preferred_element_type=F32)
    base = prefix + run_ref[:, 0:1]
    for kk in range(TOP_K):
        idx_ref[kk:kk + 1, :] = idxs[kk]
        rank_ref[kk:kk + 1, :] = jnp.sum(jnp.where(sels[kk], base, 0.0), axis=0,
                                         keepdims=True).astype(I32)
    run_ref[...] = run_ref[...] + jnp.sum(onehot, axis=1, keepdims=True)
    cnt_ref[...] = run_ref[...]


def _post(u, attn, x, p):
    B, S, _ = x.shape
    rows = min(POST_ROWS, S)
    n_i = S // rows
    T = B * S
    hb = rows // HALO
    n_halo = S // HALO
    grid = (B, n_i)
    full = lambda a: pl.BlockSpec(a.shape, lambda b, i: (0,) * a.ndim)
    row_blk = lambda w: pl.BlockSpec((None, rows, w), lambda b, i: (b, i, 0))
    tok_blk = lambda r: pl.BlockSpec((r, rows), lambda b, i: (0, b * n_i + i))
    weights = [p["dw_w"], p["dw_b"], p["ln_g"], p["ln_b"], p["w_pw2"], p["conv_out_g"],
               p["attn_out_g"], p["w_out"], p["norm2_g"], p["rw_hi"], p["rw_lo"], p["rb"], p["tri"]]
    outs = pl.pallas_call(
        _post_body,
        grid=grid,
        in_specs=[pl.BlockSpec((None, HALO, D_CONV), lambda b, i: (b, jnp.maximum(i * hb - 1, 0), 0)),
                  row_blk(D_CONV),
                  pl.BlockSpec((None, HALO, D_CONV),
                               lambda b, i: (b, jnp.minimum((i + 1) * hb, n_halo - 1), 0)),
                  row_blk(D_ATTN), row_blk(D_MODEL)] + [full(w) for w in weights],
        out_specs=[row_blk(D_MODEL),
                   pl.BlockSpec((rows, D_MODEL), lambda b, i: (b * n_i + i, 0)),
                   tok_blk(TOP_K), tok_blk(TOP_K),
                   pl.BlockSpec((rows, 8), lambda b, i: (b * n_i + i, 0)),
                   pl.BlockSpec((N_EXPERTS, LANES), lambda b, i: (0, 0))],
        out_shape=[jax.ShapeDtypeStruct((B, S, D_MODEL), F32),
                   jax.ShapeDtypeStruct((T, D_MODEL), F32),
                   jax.ShapeDtypeStruct((TOP_K, T), I32),
                   jax.ShapeDtypeStruct((TOP_K, T), I32),
                   jax.ShapeDtypeStruct((T, 8), F32),
                   jax.ShapeDtypeStruct((N_EXPERTS, LANES), F32)],
        scratch_shapes=[pltpu.VMEM((rows + 2 * HALO, D_CONV), F32),
                        pltpu.VMEM((rows, D_CONV), F32),
                        pltpu.VMEM((N_EXPERTS, LANES), F32)],
        compiler_params=_cparams(("arbitrary", "arbitrary")),
        name="post",
    )(u, u, u, attn, x, *weights)
    return outs


def _dest_body(pstart_ref, idx_ref, rank_ref, dest_ref):
    idx = idx_ref[...]
    dest = rank_ref[...]
    for e in range(N_EXPERTS):
        dest = dest + jnp.where(idx == e, pstart_ref[e], 0)
    dest_ref[...] = dest


def _dest(pstart, idx, rank, tile):
    T = idx.shape[1]
    n_t = T // tile
    return pl.pallas_call(
        _dest_body,
        grid_spec=pltpu.PrefetchScalarGridSpec(
            num_scalar_prefetch=1,
            grid=(n_t,),
            in_specs=[pl.BlockSpec((TOP_K, tile), lambda j, ps: (0, j)),
                      pl.BlockSpec((TOP_K, tile), lambda j, ps: (0, j))],
            out_specs=pl.BlockSpec((None, TOP_K, tile), lambda j, ps: (j, 0, 0)),
        ),
        out_shape=jax.ShapeDtypeStruct((n_t, TOP_K, tile), I32),
        compiler_params=_cparams(("arbitrary",)),
        name="dest",
    )(pstart, idx, rank)


def _dispatch_body(pad_lo_ref, pad_hi_ref, dest_ref, h2_ref, xs_ref, zero_ref, sem, zsem):
    j = pl.program_id(0)
    tile = h2_ref.shape[0]

    def row_copy(t, kk):
        return pltpu.make_async_copy(h2_ref.at[pl.ds(t, 1), :],
                                     xs_ref.at[pl.ds(dest_ref[kk, t], 1), :], sem)

    def zero_copy(r):
        return pltpu.make_async_copy(zero_ref.at[pl.ds(0, 1), :], xs_ref.at[pl.ds(r, 1), :], zsem)

    @pl.when(j == 0)
    def _():
        zero_ref[...] = jnp.zeros_like(zero_ref)
        for e in range(N_EXPERTS):
            lo = pad_lo_ref[e]
            hi = pad_hi_ref[e]

            def start(r, c):
                zero_copy(r).start()
                return c

            def wait(r, c):
                zero_copy(r).wait()
                return c

            lax.fori_loop(lo, hi, start, 0)
            lax.fori_loop(lo, hi, wait, 0)

    def start(t, c):
        for kk in range(TOP_K):
            row_copy(t, kk).start()
        return c

    def wait(t, c):
        for kk in range(TOP_K):
            row_copy(t, kk).wait()
        return c

    lax.fori_loop(0, tile, start, 0)
    lax.fori_loop(0, tile, wait, 0)


def _dispatch(pad_lo, pad_hi, dest, h2, n_rows):
    T = h2.shape[0]
    n_t, _, tile = dest.shape
    return pl.pallas_call(
        _dispatch_body,
        grid_spec=pltpu.PrefetchScalarGridSpec(
            num_scalar_prefetch=2,
            grid=(n_t,),
            in_specs=[pl.BlockSpec((None, TOP_K, tile), lambda j, a, b: (j, 0, 0),
                                   memory_space=pltpu.SMEM),
                      pl.BlockSpec((tile, D_MODEL), lambda j, a, b: (j, 0))],
            out_specs=pl.BlockSpec(memory_space=pl.ANY),
            scratch_shapes=[pltpu.VMEM((8, D_MODEL), F32),
                            pltpu.SemaphoreType.DMA, pltpu.SemaphoreType.DMA],
        ),
        out_shape=jax.ShapeDtypeStruct((n_rows, D_MODEL), F32),
        compiler_params=_cparams(("arbitrary",)),
        name="dispatch",
    )(pad_lo, pad_hi, dest, h2)


def _expert_body(be_ref, nact_ref, xs_ref, wgu_ref, bgu_ref, wdn_ref, bdn_ref, ys_ref):
    @pl.when(pl.program_id(0) < nact_ref[0])
    def _():
        x = xs_ref[...].astype(BF16)
        hgu = jnp.dot(x, wgu_ref[...], preferred_element_type=F32) + bgu_ref[...]
        glu = jnp.minimum(hgu[:, :D_FF], SWIGLU_LIMIT)
        lin = jnp.clip(hgu[:, D_FF:], -SWIGLU_LIMIT, SWIGLU_LIMIT)
        act = glu * jax.nn.sigmoid(SWIGLU_ALPHA * glu) * (lin + 1.0)
        ys_ref[...] = jnp.dot(act.astype(BF16), wdn_ref[...],
                              preferred_element_type=F32) + bdn_ref[...]


def _experts(block_expert, n_active, xs, w_gu, b_gu, w_dn, b_dn):
    n_rows = xs.shape[0]
    n_blocks = n_rows // EXPERT_ROWS
    blk = lambda i, be, na: (jnp.minimum(i, na[0] - 1), 0)
    per_e = lambda i, be, na: (be[i], 0, 0)
    return pl.pallas_call(
        _expert_body,
        grid_spec=pltpu.PrefetchScalarGridSpec(
            num_scalar_prefetch=2,
            grid=(n_blocks,),
            in_specs=[pl.BlockSpec((EXPERT_ROWS, D_MODEL), blk),
                      pl.BlockSpec((None, D_MODEL, 2 * D_FF), per_e),
                      pl.BlockSpec((None, 1, 2 * D_FF), per_e),
                      pl.BlockSpec((None, D_FF, D_MODEL), per_e),
                      pl.BlockSpec((None, 1, D_MODEL), per_e)],
            out_specs=pl.BlockSpec((EXPERT_ROWS, D_MODEL), blk),
        ),
        out_shape=jax.ShapeDtypeStruct((n_rows, D_MODEL), F32),
        compiler_params=_cparams(("arbitrary",)),
        name="experts",
    )(block_expert, n_active, xs, w_gu, b_gu, w_dn, b_dn)


def _combine_body(dest_ref, x1_ref, gate_ref, fg_ref, ys_ref, y_ref, buf_ref, sem):
    tile = x1_ref.shape[0]

    def row_copy(t, kk):
        return pltpu.make_async_copy(ys_ref.at[pl.ds(dest_ref[kk, t], 1), :],
                                     buf_ref.at[kk, pl.ds(t, 1), :], sem)

    def start(t, c):
        for kk in range(TOP_K):
            row_copy(t, kk).start()
        return c

    def wait(t, c):
        for kk in range(TOP_K):
            row_copy(t, kk).wait()
        return c

    lax.fori_loop(0, tile, start, 0)
    lax.fori_loop(0, tile, wait, 0)
    acc = x1_ref[...]
    for kk in range(TOP_K):
        acc = acc + gate_ref[:, kk:kk + 1] * buf_ref[kk]
    y_ref[...] = _rms(acc, fg_ref[...])


def _combine(dest, x1, gates, final_g, ys):
    T = x1.shape[0]
    n_t, _, tile = dest.shape
    return pl.pallas_call(
        _combine_body,
        grid=(n_t,),
        in_specs=[pl.BlockSpec((None, TOP_K, tile), lambda j: (j, 0, 0), memory_space=pltpu.SMEM),
                  pl.BlockSpec((tile, D_MODEL), lambda j: (j, 0)),
                  pl.BlockSpec((tile, 8), lambda j: (j, 0)),
                  pl.BlockSpec((1, D_MODEL), lambda j: (0, 0)),
                  pl.BlockSpec(memory_space=pl.ANY)],
        out_specs=pl.BlockSpec((tile, D_MODEL), lambda j: (j, 0)),
        out_shape=jax.ShapeDtypeStruct((T, D_MODEL), F32),
        scratch_shapes=[pltpu.VMEM((TOP_K, tile, D_MODEL), F32), pltpu.SemaphoreType.DMA],
        compiler_params=_cparams(("arbitrary",)),
        name="combine",
    )(dest, x1, gates, final_g, ys)


def _rope_tables(seq_len):
    t = np.arange(seq_len)
    inv = 1.0 / (ROPE_THETA ** (np.arange(0, AXIS_DIM, 2, dtype=np.float32) / AXIS_DIM))
    inv = inv.astype(np.float32)
    ang_r = (t // GRID_W).astype(np.float32)[:, None] * inv[None, :]
    ang_c = (t % GRID_W).astype(np.float32)[:, None] * inv[None, :]
    ang = np.concatenate([ang_r, ang_r, ang_c, ang_c], axis=-1).astype(np.float32)
    cos = np.cos(ang).astype(np.float32)
    sin = np.sin(ang).astype(np.float32)
    sign = np.where((np.arange(HEAD_DIM) % AXIS_DIM) < HALF_AXIS, -1.0, 1.0).astype(np.float32)
    reps = LANES // HEAD_DIM
    return jnp.asarray(np.tile(cos, (1, reps))), jnp.asarray(np.tile(sin * sign, (1, reps)))


def _prepare(norm1_g, w_in, q_norm_g, k_norm_g, conv_dw_w, conv_dw_b, conv_ln_g, conv_ln_b, w_pw2,
             attn_out_g, conv_out_g, w_out, norm2_g, router_w, router_b, w_gate_up, b_gate_up,
             w_down, b_down, final_g):
    row = lambda a: a.reshape(1, -1).astype(F32)
    qk_gain = jnp.concatenate([jnp.tile(q_norm_g[0] * (1.0 / math.sqrt(HEAD_DIM)), N_Q_HEADS),
                               jnp.tile(k_norm_g[0], N_KV_HEADS)]).reshape(1, D_QK)
    head = np.arange(LANES) // HEAD_DIM
    hsum = jnp.asarray((head[:, None] == head[None, :]).astype(np.float32), dtype=BF16)
    rw = jnp.pad(router_w[0].astype(F32), ((0, 0), (0, LANES - N_EXPERTS)))
    rw_hi = rw.astype(BF16)
    rw_lo = (rw - rw_hi.astype(F32)).astype(BF16)
    rb = jnp.pad(router_b[0].astype(F32), (0, LANES - N_EXPERTS)).reshape(1, LANES)
    tri = np.triu(np.ones((POST_ROWS, POST_ROWS), np.float32), k=1)
    wgu = w_gate_up[0]
    bgu = b_gate_up[0]
    return dict(
        norm1_g=row(norm1_g[0]), w_in=w_in[0].astype(BF16), qk_gain=qk_gain, hsum=hsum,
        dw_w=conv_dw_w[0].astype(F32), dw_b=row(conv_dw_b[0]), ln_g=row(conv_ln_g[0]),
        ln_b=row(conv_ln_b[0]), w_pw2=w_pw2[0].astype(BF16), conv_out_g=row(conv_out_g[0]),
        attn_out_g=row(attn_out_g[0]), w_out=w_out[0].astype(BF16), norm2_g=row(norm2_g[0]),
        rw_hi=rw_hi, rw_lo=rw_lo, rb=rb, tri_full=tri,
        w_gu=jnp.concatenate([wgu[:, :, 0::2], wgu[:, :, 1::2]], axis=-1).astype(BF16),
        b_gu=jnp.concatenate([bgu[:, 0::2], bgu[:, 1::2]], axis=-1).reshape(N_EXPERTS, 1, 2 * D_FF),
        w_dn=w_down[0].astype(BF16), b_dn=b_down[0].reshape(N_EXPERTS, 1, D_MODEL),
        final_g=row(final_g),
    )


def _trunk(x, p):
    B, S, _ = x.shape
    T = B * S
    cos, sin = _rope_tables(S)
    q, k, v, u = _pre(x, p["norm1_g"], p["w_in"], p["qk_gain"], cos, sin, p["hsum"])
    attn = _attention(q, k, v)
    rows = min(POST_ROWS, S)
    pp = dict(p, tri=jnp.asarray(p["tri_full"][:rows, :rows], dtype=BF16))
    x1, h2, idx, rank, gates, cnt = _post(u, attn, x, pp)

    counts = cnt[:, 0].astype(I32)
    padded = ((counts + EXPERT_ROWS - 1) // EXPERT_ROWS) * EXPERT_ROWS
    pend = jnp.cumsum(padded)
    pstart = pend - padded
    n_blocks = (T * TOP_K) // EXPERT_ROWS + N_EXPERTS
    n_active = pend[-1] // EXPERT_ROWS
    blk_start = jnp.minimum(jnp.arange(n_blocks, dtype=I32), n_active - 1) * EXPERT_ROWS
    block_expert = jnp.sum((blk_start[:, None] >= pend[None, :]).astype(I32), axis=1)
    block_expert = jnp.minimum(block_expert, N_EXPERTS - 1).astype(I32)

    tile = min(TOKEN_ROWS, T)
    dest = _dest(pstart.astype(I32), idx, rank, tile)
    xs = _dispatch((pstart + counts).astype(I32), pend.astype(I32), dest, h2, n_blocks * EXPERT_ROWS)
    ys = _experts(block_expert, n_active.reshape(1).astype(I32), xs,
                  p["w_gu"], p["b_gu"], p["w_dn"], p["b_dn"])
    y = _combine(dest, x1.reshape(T, D_MODEL), gates, p["final_g"], ys)
    return y.reshape(B, S, D_MODEL)


def kernel(x_prompt, x_sample, norm1_g, w_in, q_norm_g, k_norm_g, conv_dw_w, conv_dw_b, conv_ln_g,
           conv_ln_b, w_pw2, attn_out_g, conv_out_g, w_out, norm2_g, router_w, router_b, w_gate_up,
           b_gate_up, w_down, b_down, final_g):
    p = _prepare(norm1_g, w_in, q_norm_g, k_norm_g, conv_dw_w, conv_dw_b, conv_ln_g, conv_ln_b,
                 w_pw2, attn_out_g, conv_out_g, w_out, norm2_g, router_w, router_b, w_gate_up,
                 b_gate_up, w_down, b_down, final_g)
    return (_trunk(x_prompt, p), _trunk(x_sample, p))
```

```python
import functools
import math

import numpy as np
import jax
import jax.numpy as jnp
from jax import lax
from jax.experimental import pallas as pl
from jax.experimental.pallas import tpu as pltpu

F32 = jnp.float32
BF16 = jnp.bfloat16
I32 = jnp.int32
U32 = jnp.uint32

D_MODEL = 1024
GRID_W = 64
HEAD_DIM = 64
N_Q_HEADS = 8
N_KV_HEADS = 2
N_GROUPS_PER_KV = N_Q_HEADS // N_KV_HEADS
D_ATTN = N_Q_HEADS * HEAD_DIM
D_KV = N_KV_HEADS * HEAD_DIM
D_QK = D_ATTN + D_KV
AXIS_DIM = HEAD_DIM // 2
HALF_AXIS = AXIS_DIM // 2
ROPE_THETA = 10000.0
D_CONV = D_MODEL - D_ATTN
CONV_K = 31
CONV_PAD = (CONV_K - 1) // 2
D_IN_PROJ = D_ATTN + 2 * D_KV + 2 * D_CONV
N_EXPERTS = 32
TOP_K = 4
D_FF = D_MODEL
SWIGLU_ALPHA = 1.702
SWIGLU_LIMIT = 7.0
EPS = 1e-6

LANES = 128
SUBLANES = 8
HALO = 16
NEG_INF = float("-inf")
D_HALF = D_MODEL // 2
SEG_ALIGN = SUBLANES

PRE_ROWS = 512
ATTN_ROWS = 256
POST_ROWS = 256
EXPERT_ROWS = 256
WPREP_ROWS = 512

VMEM_LIMIT = 56 * 1024 * 1024


def _cparams(n_axes):
    return pltpu.CompilerParams(dimension_semantics=("arbitrary",) * n_axes,
                                vmem_limit_bytes=VMEM_LIMIT)


def _rms(x, g):
    return x * lax.rsqrt(jnp.mean(x * x, axis=-1, keepdims=True) + EPS) * g


def _split_bf16(x):
    hi = x.astype(BF16)
    lo = (x - hi.astype(F32)).astype(BF16)
    return hi, lo


def _pack_rows(x):
    bits = lax.bitcast_convert_type(x, U32)
    return (bits[:, :D_HALF] & jnp.uint32(0xFFFF0000)) | (bits[:, D_HALF:] >> 16)


def _unpack_rows(w):
    left = lax.bitcast_convert_type(w & jnp.uint32(0xFFFF0000), F32).astype(BF16)
    right = lax.bitcast_convert_type(w << 16, F32).astype(BF16)
    return left, right


def _wprep_body(w_ref, perm_ref, o_ref):
    perm = perm_ref[...]
    group = 2 * LANES
    for g in range(2 * D_FF // group):
        blk = w_ref[:, g * group:(g + 1) * group].astype(BF16)
        r = jnp.dot(blk, perm, preferred_element_type=F32)
        o_ref[:, g * LANES:(g + 1) * LANES] = r[:, :LANES].astype(BF16)
        o_ref[:, D_FF + g * LANES:D_FF + (g + 1) * LANES] = r[:, LANES:].astype(BF16)


def _wprep(w_gu):
    group = 2 * LANES
    i = np.arange(group)[:, None]
    c = np.arange(group)[None, :]
    perm = np.where(c < LANES, i == 2 * c, i == 2 * (c - LANES) + 1).astype(np.float32)
    return pl.pallas_call(
        _wprep_body,
        grid=(N_EXPERTS, D_MODEL // WPREP_ROWS),
        in_specs=[pl.BlockSpec((None, WPREP_ROWS, 2 * D_FF), lambda e, r: (e, r, 0)),
                  pl.BlockSpec((group, group), lambda e, r: (0, 0))],
        out_specs=pl.BlockSpec((None, WPREP_ROWS, 2 * D_FF), lambda e, r: (e, r, 0)),
        out_shape=jax.ShapeDtypeStruct((N_EXPERTS, D_MODEL, 2 * D_FF), BF16),
        compiler_params=_cparams(2),
        name="wprep",
    )(w_gu, jnp.asarray(perm, dtype=BF16))


def _pre_body(x_ref, g1_ref, win_ref, qkg_ref, cos_ref, sin_ref, hsum_ref,
              q_ref, k_ref, v_ref, u_ref):
    h = _rms(x_ref[...], g1_ref[...]).astype(BF16)
    z = jnp.dot(h, win_ref[...], preferred_element_type=F32)
    cos = cos_ref[...]
    sin = sin_ref[...]
    lane = lax.broadcasted_iota(I32, (1, LANES), 1)
    first_half = (lane % AXIS_DIM) < HALF_AXIS
    hsum = hsum_ref[...]
    for c in range(D_QK // LANES):
        zc = z[:, c * LANES:(c + 1) * LANES]
        hi, lo = _split_bf16(zc * zc)
        ss = (jnp.dot(hi, hsum, preferred_element_type=F32)
              + jnp.dot(lo, hsum, preferred_element_type=F32))
        zn = zc * lax.rsqrt(ss * (1.0 / HEAD_DIM) + EPS) * qkg_ref[:, c * LANES:(c + 1) * LANES]
        rot = jnp.where(first_half,
                        pltpu.roll(zn, LANES - HALF_AXIS, 1),
                        pltpu.roll(zn, HALF_AXIS, 1))
        out = (zn * cos + rot * sin).astype(BF16)
        if c < D_ATTN // LANES:
            q_ref[:, c * LANES:(c + 1) * LANES] = out
        else:
            k_ref[...] = out
    v_ref[...] = z[:, D_QK:D_QK + D_KV].astype(BF16)
    a = z[:, D_QK + D_KV:D_QK + D_KV + D_CONV]
    gate = z[:, D_QK + D_KV + D_CONV:]
    u_ref[...] = a * jax.nn.sigmoid(gate)


def _pre(x, g1, w_in, qk_gain, cos, sin, hsum):
    B, S, _ = x.shape
    rows = min(PRE_ROWS, S)
    grid = (B, S // rows)
    row_blk = lambda w: pl.BlockSpec((None, rows, w), lambda b, i: (b, i, 0))
    full = lambda a: pl.BlockSpec(a.shape, lambda b, i: (0,) * a.ndim)
    return pl.pallas_call(
        _pre_body,
        grid=grid,
        in_specs=[row_blk(D_MODEL), full(g1), full(w_in), full(qk_gain),
                  pl.BlockSpec((rows, LANES), lambda b, i: (i, 0)),
                  pl.BlockSpec((rows, LANES), lambda b, i: (i, 0)),
                  full(hsum)],
        out_specs=[row_blk(D_ATTN), row_blk(D_KV), row_blk(D_KV), row_blk(D_CONV)],
        out_shape=[jax.ShapeDtypeStruct((B, S, D_ATTN), BF16),
                   jax.ShapeDtypeStruct((B, S, D_KV), BF16),
                   jax.ShapeDtypeStruct((B, S, D_KV), BF16),
                   jax.ShapeDtypeStruct((B, S, D_CONV), F32)],
        compiler_params=_cparams(2),
        name="pre",
    )(x, g1, w_in, qk_gain, cos, sin, hsum)


def _attn_body(q_ref, k_ref, v_ref, o_ref):
    nt = (((1,), (1,)), ((), ()))
    for kv in range(N_KV_HEADS):
        kh = k_ref[:, kv * HEAD_DIM:(kv + 1) * HEAD_DIM]
        vh = v_ref[:, kv * HEAD_DIM:(kv + 1) * HEAD_DIM]
        for g in range(N_GROUPS_PER_KV):
            h = kv * N_GROUPS_PER_KV + g
            qh = q_ref[:, h * HEAD_DIM:(h + 1) * HEAD_DIM]
            s = lax.dot_general(qh, kh, nt, preferred_element_type=F32)
            m = jnp.max(s, axis=-1, keepdims=True)
            p = jnp.exp(s - m)
            l = jnp.sum(p, axis=-1, keepdims=True)
            o = jnp.dot(p.astype(BF16), vh, preferred_element_type=F32)
            o_ref[:, h * HEAD_DIM:(h + 1) * HEAD_DIM] = (o / l).astype(BF16)


def _attention(q, k, v):
    B, S, _ = q.shape
    rows = min(ATTN_ROWS, S)
    return pl.pallas_call(
        _attn_body,
        grid=(B, S // rows),
        in_specs=[pl.BlockSpec((None, rows, D_ATTN), lambda b, i: (b, i, 0)),
                  pl.BlockSpec((None, S, D_KV), lambda b, i: (b, 0, 0)),
                  pl.BlockSpec((None, S, D_KV), lambda b, i: (b, 0, 0))],
        out_specs=pl.BlockSpec((None, rows, D_ATTN), lambda b, i: (b, i, 0)),
        out_shape=jax.ShapeDtypeStruct((B, S, D_ATTN), BF16),
        compiler_params=_cparams(2),
        name="attn",
    )(q, k, v)


def _top4(lg, iota):
    vals, sels = [], []
    for _ in range(TOP_K):
        m = jnp.max(lg, axis=0, keepdims=True)
        idx = jnp.min(jnp.where(lg == m, iota, N_EXPERTS), axis=0, keepdims=True)
        sel = iota == idx
        vals.append(m)
        sels.append(sel)
        lg = jnp.where(sel, NEG_INF, lg)
    return vals, sels


def _post_body(up_ref, um_ref, un_ref, attn_ref, x_ref,
               dww_ref, dwb_ref, lng_ref, lnb_ref, wpw_ref, cog_ref, aog_ref, wout_ref,
               g2_ref, rwh_ref, rwl_ref, rb_ref, tri_ref, low_ref,
               x1_ref, h2_ref, lpos_ref, gate_ref, seg_ref, cnt_ref,
               wpad_ref, conv_ref, run_ref):
    b = pl.program_id(0)
    i = pl.program_id(1)
    n_i = pl.num_programs(1)
    rows = um_ref.shape[0]

    @pl.when((b == 0) & (i == 0))
    def _():
        run_ref[...] = jnp.zeros_like(run_ref)

    wpad_ref[0:HALO, :] = jnp.where(i > 0, up_ref[...], 0.0)
    wpad_ref[HALO:HALO + rows, :] = um_ref[...]
    wpad_ref[HALO + rows:, :] = jnp.where(i < n_i - 1, un_ref[...], 0.0)
    for c in range(D_CONV // LANES):
        cs = slice(c * LANES, (c + 1) * LANES)
        acc = jnp.zeros((rows, LANES), F32)
        for j in range(CONV_K):
            off = HALO - CONV_PAD + j
            acc = acc + wpad_ref[off:off + rows, cs] * dww_ref[j:j + 1, cs]
        conv_ref[:, cs] = acc + dwb_ref[:, cs]
    y = conv_ref[...]
    mu = jnp.mean(y, axis=-1, keepdims=True)
    yc = y - mu
    var = jnp.mean(yc * yc, axis=-1, keepdims=True)
    y = yc * lax.rsqrt(var + EPS) * lng_ref[...] + lnb_ref[...]
    y = y * jax.nn.sigmoid(y)
    conv_out = jnp.dot(y.astype(BF16), wpw_ref[...], preferred_element_type=F32)

    cn = _rms(conv_out, cog_ref[...]).astype(BF16)
    an = _rms(attn_ref[...].astype(F32), aog_ref[...]).astype(BF16)
    mix = (jnp.dot(cn, wout_ref[0:D_CONV, :], preferred_element_type=F32)
           + jnp.dot(an, wout_ref[D_CONV:, :], preferred_element_type=F32))
    x1 = x_ref[...] + mix
    x1_ref[...] = x1
    h2 = _rms(x1, g2_ref[...])
    hi, lo = _split_bf16(h2)
    h2_ref[...] = hi

    rwh = rwh_ref[...]
    lg = (jnp.dot(hi, rwh, preferred_element_type=F32)
          + jnp.dot(lo, rwh, preferred_element_type=F32)
          + jnp.dot(hi, rwl_ref[...], preferred_element_type=F32)) + rb_ref[...]

    lgt = lg.T[0:N_EXPERTS, :]
    eio = lax.broadcasted_iota(I32, (N_EXPERTS, rows), 0)
    vals, sels = _top4(lgt, eio)
    es = [jnp.exp(v - vals[0]) for v in vals]
    inv = 1.0 / (es[0] + es[1] + es[2] + es[3])
    onehot = jnp.where(sels[0] | sels[1] | sels[2] | sels[3], 1.0, 0.0)
    prefix = jnp.dot(onehot.astype(BF16), tri_ref[...], preferred_element_type=F32)
    n_tok = jnp.sum(onehot, axis=1, keepdims=True)
    n_seg = jnp.floor((n_tok + (SEG_ALIGN - 1)) * (1.0 / SEG_ALIGN)) * SEG_ALIGN
    off = jnp.dot(low_ref[...], jnp.broadcast_to(n_seg, (N_EXPERTS, LANES)).astype(BF16),
                  preferred_element_type=F32)
    pos = off[:, 0:1] + prefix
    for kk in range(TOP_K):
        gate_ref[kk:kk + 1, :] = es[kk] * inv
        lpos_ref[kk:kk + 1, :] = jnp.sum(jnp.where(sels[kk], pos, 0.0), axis=0,
                                         keepdims=True).astype(I32)
    lane = lax.broadcasted_iota(I32, (N_EXPERTS, LANES), 1)
    seg_ref[...] = jnp.where(lane == 0, n_seg, jnp.where(lane == 1, run_ref[...],
                                                         jnp.where(lane == 2, off, 0.0)))
    run_ref[...] = run_ref[...] + n_seg
    cnt_ref[...] = run_ref[...]


def _post(u, attn, x, p):
    B, S, _ = x.shape
    rows = min(POST_ROWS, S)
    n_i = S // rows
    T = B * S
    hb = rows // HALO
    n_halo = S // HALO
    grid = (B, n_i)
    full = lambda a: pl.BlockSpec(a.shape, lambda b, i: (0,) * a.ndim)
    row_blk = lambda w: pl.BlockSpec((None, rows, w), lambda b, i: (b, i, 0))
    tok_blk = lambda r: pl.BlockSpec((r, rows), lambda b, i: (0, b * n_i + i))
    weights = [p["dw_w"], p["dw_b"], p["ln_g"], p["ln_b"], p["w_pw2"], p["conv_out_g"],
               p["attn_out_g"], p["w_out"], p["norm2_g"], p["rw_hi"], p["rw_lo"], p["rb"],
               p["tri"], p["low"]]
    return pl.pallas_call(
        _post_body,
        grid=grid,
        in_specs=[pl.BlockSpec((None, HALO, D_CONV), lambda b, i: (b, jnp.maximum(i * hb - 1, 0), 0)),
                  row_blk(D_CONV),
                  pl.BlockSpec((None, HALO, D_CONV),
                               lambda b, i: (b, jnp.minimum((i + 1) * hb, n_halo - 1), 0)),
                  row_blk(D_ATTN), row_blk(D_MODEL)] + [full(w) for w in weights],
        out_specs=[row_blk(D_MODEL),
                   pl.BlockSpec((rows, D_MODEL), lambda b, i: (b * n_i + i, 0)),
                   tok_blk(TOP_K), tok_blk(TOP_K),
                   pl.BlockSpec((None, N_EXPERTS, LANES), lambda b, i: (b * n_i + i, 0, 0)),
                   pl.BlockSpec((N_EXPERTS, LANES), lambda b, i: (0, 0))],
        out_shape=[jax.ShapeDtypeStruct((B, S, D_MODEL), F32),
                   jax.ShapeDtypeStruct((T, D_MODEL), BF16),
                   jax.ShapeDtypeStruct((TOP_K, T), I32),
                   jax.ShapeDtypeStruct((TOP_K, T), F32),
                   jax.ShapeDtypeStruct((B * n_i, N_EXPERTS, LANES), F32),
                   jax.ShapeDtypeStruct((N_EXPERTS, LANES), F32)],
        scratch_shapes=[pltpu.VMEM((rows + 2 * HALO, D_CONV), F32),
                        pltpu.VMEM((rows, D_CONV), F32),
                        pltpu.VMEM((N_EXPERTS, LANES), F32)],
        compiler_params=_cparams(2),
        name="post",
    )(u, u, u, attn, x, *weights)


def _segments(n_ref, tile, fn):
    for e in range(N_EXPERTS):
        n = n_ref[tile * N_EXPERTS + e]

        @pl.when(n > 0)
        def _():
            fn(e, pl.multiple_of(n, SEG_ALIGN))


def _slot_matrix(lpos_ref, vals, n_slots):
    tile = lpos_ref.shape[1]
    sio = lax.broadcasted_iota(I32, (n_slots, tile), 0)
    m = jnp.zeros((n_slots, tile), F32)
    for kk in range(TOP_K):
        v = 1.0 if vals is None else vals[kk:kk + 1, :]
        m = jnp.where(sio == lpos_ref[kk:kk + 1, :], v, m)
    return m


def _dispatch_body(n_ref, off_ref, dst_ref, tlo_ref, tn_ref,
                   lpos_ref, h2_ref, xs_ref, buf_ref, zero_ref, sem, zsem):
    j = pl.program_id(0)
    n_j = pl.num_programs(0)
    slot = j % 2
    n_slots = buf_ref.shape[1]

    def seg_copy(tile, sl, e, n):
        src = buf_ref.at[sl, pl.ds(pl.multiple_of(off_ref[tile * N_EXPERTS + e], SEG_ALIGN), n), :]
        dst = xs_ref.at[pl.ds(pl.multiple_of(dst_ref[tile * N_EXPERTS + e], SEG_ALIGN), n), :]
        return pltpu.make_async_copy(src, dst, sem.at[sl])

    def tail_copy(e, n):
        return pltpu.make_async_copy(
            zero_ref.at[pl.ds(0, n), :],
            xs_ref.at[pl.ds(pl.multiple_of(tlo_ref[e], SEG_ALIGN), n), :], zsem)

    @pl.when(j == 0)
    def _():
        zero_ref[...] = jnp.zeros_like(zero_ref)
        _segments(tn_ref, 0, lambda e, n: tail_copy(e, n).start())
        _segments(tn_ref, 0, lambda e, n: tail_copy(e, n).wait())

    @pl.when(j >= 2)
    def _():
        _segments(n_ref, j - 2, lambda e, n: seg_copy(j - 2, slot, e, n).wait())

    sel = _slot_matrix(lpos_ref, None, n_slots).astype(BF16)
    rows = jnp.dot(sel, h2_ref[...], preferred_element_type=F32)
    buf_ref[slot] = _pack_rows(rows)
    _segments(n_ref, j, lambda e, n: seg_copy(j, slot, e, n).start())

    @pl.when(j == n_j - 1)
    def _():
        @pl.when(j >= 1)
        def _():
            _segments(n_ref, j - 1, lambda e, n: seg_copy(j - 1, 1 - slot, e, n).wait())
        _segments(n_ref, j, lambda e, n: seg_copy(j, slot, e, n).wait())


def _dispatch(seg_n, seg_off, seg_dst, tail_lo, tail_n, lpos, h2, n_rows):
    tile = min(POST_ROWS, h2.shape[0])
    n_t = h2.shape[0] // tile
    n_slots = TOP_K * tile + N_EXPERTS * SEG_ALIGN
    return pl.pallas_call(
        _dispatch_body,
        grid_spec=pltpu.PrefetchScalarGridSpec(
            num_scalar_prefetch=5,
            grid=(n_t,),
            in_specs=[pl.BlockSpec((TOP_K, tile), lambda j, *_: (0, j)),
                      pl.BlockSpec((tile, D_MODEL), lambda j, *_: (j, 0))],
            out_specs=pl.BlockSpec(memory_space=pl.ANY),
            scratch_shapes=[pltpu.VMEM((2, n_slots, D_HALF), U32),
                            pltpu.VMEM((EXPERT_ROWS, D_HALF), U32),
                            pltpu.SemaphoreType.DMA((2,)), pltpu.SemaphoreType.DMA],
        ),
        out_shape=jax.ShapeDtypeStruct((n_rows, D_HALF), U32),
        compiler_params=_cparams(1),
        name="dispatch",
    )(seg_n, seg_off, seg_dst, tail_lo, tail_n, lpos, h2)


def _combine_body(n_ref, off_ref, dst_ref, lpos_ref, gate_ref, x1_ref, fg_ref, ys_ref,
                  y_ref, buf_ref, sem):
    j = pl.program_id(0)
    n_j = pl.num_programs(0)
    slot = j % 2
    n_slots = buf_ref.shape[1]

    def seg_copy(tile, sl, e, n):
        src = ys_ref.at[pl.ds(pl.multiple_of(dst_ref[tile * N_EXPERTS + e], SEG_ALIGN), n), :]
        dst = buf_ref.at[sl, pl.ds(pl.multiple_of(off_ref[tile * N_EXPERTS + e], SEG_ALIGN), n), :]
        return pltpu.make_async_copy(src, dst, sem.at[sl])

    @pl.when(j == 0)
    def _():
        buf_ref[...] = jnp.zeros_like(buf_ref)
        _segments(n_ref, 0, lambda e, n: seg_copy(0, 0, e, n).start())

    @pl.when(j + 1 < n_j)
    def _():
        _segments(n_ref, j + 1, lambda e, n: seg_copy(j + 1, 1 - slot, e, n).start())

    _segments(n_ref, j, lambda e, n: seg_copy(j, slot, e, n).wait())

    w = _slot_matrix(lpos_ref, gate_ref, n_slots).T
    w_hi, w_lo = _split_bf16(w)
    left, right = _unpack_rows(buf_ref[slot])
    moe = jnp.concatenate(
        [jnp.dot(w_hi, half, preferred_element_type=F32) + jnp.dot(w_lo, half, preferred_element_type=F32)
         for half in (left, right)], axis=1)
    y_ref[...] = _rms(x1_ref[...] + moe, fg_ref[...])


def _combine(seg_n, seg_off, seg_dst, lpos, gates, x1, final_g, ys):
    T = x1.shape[0]
    tile = min(POST_ROWS, T)
    n_t = T // tile
    n_slots = TOP_K * tile + N_EXPERTS * SEG_ALIGN
    return pl.pallas_call(
        _combine_body,
        grid_spec=pltpu.PrefetchScalarGridSpec(
            num_scalar_prefetch=3,
            grid=(n_t,),
            in_specs=[pl.BlockSpec((TOP_K, tile), lambda j, *_: (0, j)),
                      pl.BlockSpec((TOP_K, tile), lambda j, *_: (0, j)),
                      pl.BlockSpec((tile, D_MODEL), lambda j, *_: (j, 0)),
                      pl.BlockSpec((1, D_MODEL), lambda j, *_: (0, 0)),
                      pl.BlockSpec(memory_space=pl.ANY)],
            out_specs=pl.BlockSpec((tile, D_MODEL), lambda j, *_: (j, 0)),
            scratch_shapes=[pltpu.VMEM((2, n_slots, D_HALF), U32), pltpu.SemaphoreType.DMA((2,))],
        ),
        out_shape=jax.ShapeDtypeStruct((T, D_MODEL), F32),
        compiler_params=_cparams(1),
        name="combine",
    )(seg_n, seg_off, seg_dst, lpos, gates, x1, final_g, ys)


def _expert_body(be_ref, nact_ref, xs_ref, wgu_ref, bgu_ref, wdn_ref, bdn_ref, ys_ref):
    @pl.when(pl.program_id(0) < nact_ref[0])
    def _():
        left, right = _unpack_rows(xs_ref[...])
        hgu = (jnp.dot(left, wgu_ref[0:D_HALF, :], preferred_element_type=F32)
               + jnp.dot(right, wgu_ref[D_HALF:, :], preferred_element_type=F32)) + bgu_ref[...]
        glu = jnp.minimum(hgu[:, :D_FF], SWIGLU_LIMIT)
        lin = jnp.clip(hgu[:, D_FF:], -SWIGLU_LIMIT, SWIGLU_LIMIT)
        act = glu * jax.nn.sigmoid(SWIGLU_ALPHA * glu) * (lin + 1.0)
        y = jnp.dot(act.astype(BF16), wdn_ref[...], preferred_element_type=F32) + bdn_ref[...]
        ys_ref[...] = _pack_rows(y.astype(BF16).astype(F32))


def _experts(block_expert, n_active, xs, w_gu, b_gu, w_dn, b_dn):
    n_rows = xs.shape[0]
    n_blocks = n_rows // EXPERT_ROWS
    blk = lambda i, be, na: (jnp.minimum(i, na[0] - 1), 0)
    per_e = lambda i, be, na: (be[i], 0, 0)
    return pl.pallas_call(
        _expert_body,
        grid_spec=pltpu.PrefetchScalarGridSpec(
            num_scalar_prefetch=2,
            grid=(n_blocks,),
            in_specs=[pl.BlockSpec((EXPERT_ROWS, D_HALF), blk),
                      pl.BlockSpec((None, D_MODEL, 2 * D_FF), per_e),
                      pl.BlockSpec((None, 1, 2 * D_FF), per_e),
                      pl.BlockSpec((None, D_FF, D_MODEL), per_e),
                      pl.BlockSpec((None, 1, D_MODEL), per_e)],
            out_specs=pl.BlockSpec((EXPERT_ROWS, D_HALF), blk),
        ),
        out_shape=jax.ShapeDtypeStruct((n_rows, D_HALF), U32),
        compiler_params=_cparams(1),
        name="experts",
    )(block_expert, n_active, xs, w_gu, b_gu, w_dn, b_dn)


def _rope_tables(seq_len):
    t = np.arange(seq_len)
    inv = 1.0 / (ROPE_THETA ** (np.arange(0, AXIS_DIM, 2, dtype=np.float32) / AXIS_DIM))
    inv = inv.astype(np.float32)
    ang_r = (t // GRID_W).astype(np.float32)[:, None] * inv[None, :]
    ang_c = (t % GRID_W).astype(np.float32)[:, None] * inv[None, :]
    ang = np.concatenate([ang_r, ang_r, ang_c, ang_c], axis=-1).astype(np.float32)
    cos = np.cos(ang).astype(np.float32)
    sin = np.sin(ang).astype(np.float32)
    sign = np.where((np.arange(HEAD_DIM) % AXIS_DIM) < HALF_AXIS, -1.0, 1.0).astype(np.float32)
    reps = LANES // HEAD_DIM
    return jnp.asarray(np.tile(cos, (1, reps))), jnp.asarray(np.tile(sin * sign, (1, reps)))


def _prepare(norm1_g, w_in, q_norm_g, k_norm_g, conv_dw_w, conv_dw_b, conv_ln_g, conv_ln_b, w_pw2,
             attn_out_g, conv_out_g, w_out, norm2_g, router_w, router_b, w_gate_up, b_gate_up,
             w_down, b_down, final_g):
    row = lambda a: a.reshape(1, -1).astype(F32)
    qk_gain = jnp.concatenate([jnp.tile(q_norm_g[0] * (1.0 / math.sqrt(HEAD_DIM)), N_Q_HEADS),
                               jnp.tile(k_norm_g[0], N_KV_HEADS)]).reshape(1, D_QK)
    head = np.arange(LANES) // HEAD_DIM
    hsum = jnp.asarray((head[:, None] == head[None, :]).astype(np.float32), dtype=BF16)
    rw = jnp.pad(router_w[0].astype(F32), ((0, 0), (0, LANES - N_EXPERTS)))
    rw_hi = rw.astype(BF16)
    rw_lo = (rw - rw_hi.astype(F32)).astype(BF16)
    rb = jnp.pad(router_b[0].astype(F32), (0, LANES - N_EXPERTS)).reshape(1, LANES)
    low = np.tril(np.ones((N_EXPERTS, N_EXPERTS), np.float32), k=-1)
    bgu = b_gate_up[0]
    return dict(
        norm1_g=row(norm1_g[0]), w_in=w_in[0].astype(BF16), qk_gain=qk_gain, hsum=hsum,
        dw_w=conv_dw_w[0].astype(F32), dw_b=row(conv_dw_b[0]), ln_g=row(conv_ln_g[0]),
        ln_b=row(conv_ln_b[0]), w_pw2=w_pw2[0].astype(BF16), conv_out_g=row(conv_out_g[0]),
        attn_out_g=row(attn_out_g[0]), w_out=w_out[0].astype(BF16), norm2_g=row(norm2_g[0]),
        rw_hi=rw_hi, rw_lo=rw_lo, rb=rb, low=jnp.asarray(low, dtype=BF16),
        w_gu=_wprep(w_gate_up[0]),
        b_gu=jnp.concatenate([bgu[:, 0::2], bgu[:, 1::2]], axis=-1).reshape(N_EXPERTS, 1, 2 * D_FF),
        w_dn=w_down[0].astype(BF16), b_dn=b_down[0].reshape(N_EXPERTS, 1, D_MODEL),
        final_g=row(final_g),
    )


def _trunk(x, p):
    B, S, _ = x.shape
    T = B * S
    cos, sin = _rope_tables(S)
    q, k, v, u = _pre(x, p["norm1_g"], p["w_in"], p["qk_gain"], cos, sin, p["hsum"])
    attn = _attention(q, k, v)
    tile = min(POST_ROWS, S)
    tri = jnp.asarray(np.triu(np.ones((tile, tile), np.float32), k=1), dtype=BF16)
    x1, h2, lpos, gates, seg, cnt = _post(u, attn, x, dict(p, tri=tri))

    n_t = T // tile
    seg = seg[:, :, 0:3].astype(I32)
    seg_n, seg_base, seg_off = seg[:, :, 0], seg[:, :, 1], seg[:, :, 2]
    counts = cnt[:, 0].astype(I32)
    padded = ((counts + EXPERT_ROWS - 1) // EXPERT_ROWS) * EXPERT_ROWS
    pend = jnp.cumsum(padded)
    pstart = pend - padded
    seg_dst = pstart[None, :] + seg_base
    n_blocks = (T * TOP_K + n_t * N_EXPERTS * SEG_ALIGN) // EXPERT_ROWS + N_EXPERTS
    n_active = pend[-1] // EXPERT_ROWS
    blk_start = jnp.minimum(jnp.arange(n_blocks, dtype=I32), n_active - 1) * EXPERT_ROWS
    block_expert = jnp.sum((blk_start[:, None] >= pend[None, :]).astype(I32), axis=1)
    block_expert = jnp.minimum(block_expert, N_EXPERTS - 1).astype(I32)
    flat = lambda a: a.reshape(-1).astype(I32)

    xs = _dispatch(flat(seg_n), flat(seg_off), flat(seg_dst), flat(pstart + counts),
                   flat(padded - counts), lpos, h2, n_blocks * EXPERT_ROWS)
    ys = _experts(block_expert, n_active.reshape(1).astype(I32), xs,
                  p["w_gu"], p["b_gu"], p["w_dn"], p["b_dn"])
    y = _combine(flat(seg_n), flat(seg_off), flat(seg_dst), lpos, gates,
                 x1.reshape(T, D_MODEL), p["final_g"], ys)
    return y.reshape(B, S, D_MODEL)


def kernel(x_prompt, x_sample, norm1_g, w_in, q_norm_g, k_norm_g, conv_dw_w, conv_dw_b, conv_ln_g,
           conv_ln_b, w_pw2, attn_out_g, conv_out_g, w_out, norm2_g, router_w, router_b, w_gate_up,
           b_gate_up, w_down, b_down, final_g):
    p = _prepare(norm1_g, w_in, q_norm_g, k_norm_g, conv_dw_w, conv_dw_b, conv_ln_g, conv_ln_b,
                 w_pw2, attn_out_g, conv_out_g, w_out, norm2_g, router_w, router_b, w_gate_up,
                 b_gate_up, w_down, b_down, final_g)
    return (_trunk(x_prompt, p), _trunk(x_sample, p))
```

```python
import functools
import math

import numpy as np
import jax
import jax.numpy as jnp
from jax import lax
from jax.experimental import pallas as pl
from jax.experimental.pallas import tpu as pltpu

F32 = jnp.float32
BF16 = jnp.bfloat16
I32 = jnp.int32
U32 = jnp.uint32

D_MODEL = 1024
GRID_W = 64
HEAD_DIM = 64
N_Q_HEADS = 8
N_KV_HEADS = 2
N_GROUPS_PER_KV = N_Q_HEADS // N_KV_HEADS
D_ATTN = N_Q_HEADS * HEAD_DIM
D_KV = N_KV_HEADS * HEAD_DIM
D_QK = D_ATTN + D_KV
AXIS_DIM = HEAD_DIM // 2
HALF_AXIS = AXIS_DIM // 2
ROPE_THETA = 10000.0
D_CONV = D_MODEL - D_ATTN
CONV_K = 31
CONV_PAD = (CONV_K - 1) // 2
D_IN_PROJ = D_ATTN + 2 * D_KV + 2 * D_CONV
N_EXPERTS = 32
TOP_K = 4
D_FF = D_MODEL
SWIGLU_ALPHA = 1.702
SWIGLU_LIMIT = 7.0
EPS = 1e-6

LANES = 128
SUBLANES = 8
HALO = 16
NEG_INF = float("-inf")
D_HALF = D_MODEL // 2
SEG_ALIGN = SUBLANES

PRE_ROWS = 512
ATTN_ROWS = 256
POST_ROWS = 256
EXPERT_ROWS = 512
WPREP_ROWS = 512

VMEM_LIMIT = 56 * 1024 * 1024


def _cparams(n_axes):
    return pltpu.CompilerParams(dimension_semantics=("arbitrary",) * n_axes,
                                vmem_limit_bytes=VMEM_LIMIT)


def _rms(x, g):
    return x * lax.rsqrt(jnp.mean(x * x, axis=-1, keepdims=True) + EPS) * g


def _split_bf16(x):
    hi = x.astype(BF16)
    lo = (x - hi.astype(F32)).astype(BF16)
    return hi, lo


def _pack_rows(x):
    bits = lax.bitcast_convert_type(x, U32)
    return (bits[:, :D_HALF] & jnp.uint32(0xFFFF0000)) | (bits[:, D_HALF:] >> 16)


def _unpack_rows(w):
    left = lax.bitcast_convert_type(w & jnp.uint32(0xFFFF0000), F32).astype(BF16)
    right = lax.bitcast_convert_type(w << 16, F32).astype(BF16)
    return left, right


def _wprep_body(w_ref, perm_ref, o_ref):
    perm = perm_ref[...]
    group = 2 * LANES
    for g in range(2 * D_FF // group):
        blk = w_ref[:, g * group:(g + 1) * group].astype(BF16)
        r = jnp.dot(blk, perm, preferred_element_type=F32)
        o_ref[:, g * LANES:(g + 1) * LANES] = r[:, :LANES].astype(BF16)
        o_ref[:, D_FF + g * LANES:D_FF + (g + 1) * LANES] = r[:, LANES:].astype(BF16)


def _wprep(w_gu):
    group = 2 * LANES
    i = np.arange(group)[:, None]
    c = np.arange(group)[None, :]
    perm = np.where(c < LANES, i == 2 * c, i == 2 * (c - LANES) + 1).astype(np.float32)
    return pl.pallas_call(
        _wprep_body,
        grid=(N_EXPERTS, D_MODEL // WPREP_ROWS),
        in_specs=[pl.BlockSpec((None, WPREP_ROWS, 2 * D_FF), lambda e, r: (e, r, 0)),
                  pl.BlockSpec((group, group), lambda e, r: (0, 0))],
        out_specs=pl.BlockSpec((None, WPREP_ROWS, 2 * D_FF), lambda e, r: (e, r, 0)),
        out_shape=jax.ShapeDtypeStruct((N_EXPERTS, D_MODEL, 2 * D_FF), BF16),
        compiler_params=_cparams(2),
        name="wprep",
    )(w_gu, jnp.asarray(perm, dtype=BF16))


def _pre_body(x_ref, g1_ref, win_ref, qkg_ref, cos_ref, sin_ref, hsum_ref,
              q_ref, k_ref, v_ref, u_ref):
    h = _rms(x_ref[...], g1_ref[...]).astype(BF16)
    z = jnp.dot(h, win_ref[...], preferred_element_type=F32)
    cos = cos_ref[...]
    sin = sin_ref[...]
    lane = lax.broadcasted_iota(I32, (1, LANES), 1)
    first_half = (lane % AXIS_DIM) < HALF_AXIS
    hsum = hsum_ref[...]
    for c in range(D_QK // LANES):
        zc = z[:, c * LANES:(c + 1) * LANES]
        hi, lo = _split_bf16(zc * zc)
        ss = (jnp.dot(hi, hsum, preferred_element_type=F32)
              + jnp.dot(lo, hsum, preferred_element_type=F32))
        zn = zc * lax.rsqrt(ss * (1.0 / HEAD_DIM) + EPS) * qkg_ref[:, c * LANES:(c + 1) * LANES]
        rot = jnp.where(first_half,
                        pltpu.roll(zn, LANES - HALF_AXIS, 1),
                        pltpu.roll(zn, HALF_AXIS, 1))
        out = (zn * cos + rot * sin).astype(BF16)
        if c < D_ATTN // LANES:
            q_ref[:, c * LANES:(c + 1) * LANES] = out
        else:
            k_ref[...] = out
    zv = z[:, D_QK:D_QK + D_KV]
    ones_col = jnp.where(lane == HEAD_DIM, 1.0, 0.0)
    for kv in range(N_KV_HEADS):
        zk = zv if kv == 0 else pltpu.roll(zv, LANES - kv * HEAD_DIM, 1)
        v_ref[:, kv * LANES:(kv + 1) * LANES] = jnp.where(lane < HEAD_DIM, zk, ones_col).astype(BF16)
    a = z[:, D_QK + D_KV:D_QK + D_KV + D_CONV]
    gate = z[:, D_QK + D_KV + D_CONV:]
    u_ref[...] = a * jax.nn.sigmoid(gate)


def _pre(x, g1, w_in, qk_gain, cos, sin, hsum):
    B, S, _ = x.shape
    rows = min(PRE_ROWS, S)
    grid = (B, S // rows)
    row_blk = lambda w: pl.BlockSpec((None, rows, w), lambda b, i: (b, i, 0))
    full = lambda a: pl.BlockSpec(a.shape, lambda b, i: (0,) * a.ndim)
    return pl.pallas_call(
        _pre_body,
        grid=grid,
        in_specs=[row_blk(D_MODEL), full(g1), full(w_in), full(qk_gain),
                  pl.BlockSpec((rows, LANES), lambda b, i: (i, 0)),
                  pl.BlockSpec((rows, LANES), lambda b, i: (i, 0)),
                  full(hsum)],
        out_specs=[row_blk(D_ATTN), row_blk(D_KV), row_blk(N_KV_HEADS * LANES), row_blk(D_CONV)],
        out_shape=[jax.ShapeDtypeStruct((B, S, D_ATTN), BF16),
                   jax.ShapeDtypeStruct((B, S, D_KV), BF16),
                   jax.ShapeDtypeStruct((B, S, N_KV_HEADS * LANES), BF16),
                   jax.ShapeDtypeStruct((B, S, D_CONV), F32)],
        compiler_params=_cparams(2),
        name="pre",
    )(x, g1, w_in, qk_gain, cos, sin, hsum)


def _attn_body(q_ref, k_ref, v_ref, o_ref):
    nt = (((1,), (1,)), ((), ()))
    for kv in range(N_KV_HEADS):
        kh = k_ref[:, kv * HEAD_DIM:(kv + 1) * HEAD_DIM]
        vh = v_ref[:, kv * LANES:(kv + 1) * LANES]
        for g in range(N_GROUPS_PER_KV):
            h = kv * N_GROUPS_PER_KV + g
            qh = q_ref[:, h * HEAD_DIM:(h + 1) * HEAD_DIM]
            s = lax.dot_general(qh, kh, nt, preferred_element_type=F32)
            m = jnp.max(s, axis=-1, keepdims=True)
            p = jnp.exp2(s - m).astype(BF16)
            o = jnp.dot(p, vh, preferred_element_type=F32)
            o_ref[:, h * HEAD_DIM:(h + 1) * HEAD_DIM] = (
                o[:, :HEAD_DIM] / o[:, HEAD_DIM:HEAD_DIM + 1]).astype(BF16)


def _attention(q, k, v):
    B, S, _ = q.shape
    rows = min(ATTN_ROWS, S)
    return pl.pallas_call(
        _attn_body,
        grid=(B, S // rows),
        in_specs=[pl.BlockSpec((None, rows, D_ATTN), lambda b, i: (b, i, 0)),
                  pl.BlockSpec((None, S, D_KV), lambda b, i: (b, 0, 0)),
                  pl.BlockSpec((None, S, N_KV_HEADS * LANES), lambda b, i: (b, 0, 0))],
        out_specs=pl.BlockSpec((None, rows, D_ATTN), lambda b, i: (b, i, 0)),
        out_shape=jax.ShapeDtypeStruct((B, S, D_ATTN), BF16),
        compiler_params=_cparams(2),
        name="attn",
    )(q, k, v)


def _top4(lg, iota):
    vals, sels = [], []
    for _ in range(TOP_K):
        m = jnp.max(lg, axis=0, keepdims=True)
        idx = jnp.min(jnp.where(lg == m, iota, N_EXPERTS), axis=0, keepdims=True)
        sel = iota == idx
        vals.append(m)
        sels.append(sel)
        lg = jnp.where(sel, NEG_INF, lg)
    return vals, sels


def _post_body(up_ref, um_ref, un_ref, attn_ref, x_ref,
               dww_ref, dwb_ref, lng_ref, lnb_ref, wpw_ref, cog_ref, aog_ref, wout_ref,
               g2_ref, rwh_ref, rwl_ref, rb_ref, tri_ref, low_ref,
               x1_ref, h2_ref, lpos_ref, gate_ref, seg_ref, cnt_ref,
               wpad_ref, shift_ref, conv_ref, run_ref):
    b = pl.program_id(0)
    i = pl.program_id(1)
    n_i = pl.num_programs(1)
    rows = um_ref.shape[0]

    @pl.when((b == 0) & (i == 0))
    def _():
        run_ref[...] = jnp.zeros_like(run_ref)

    wpad_ref[0:HALO, :] = jnp.where(i > 0, up_ref[...], 0.0)
    wpad_ref[HALO:HALO + rows, :] = um_ref[...]
    wpad_ref[HALO + rows:, :] = jnp.where(i < n_i - 1, un_ref[...], 0.0)
    span = rows + 2 * HALO - SUBLANES
    for r in range(1, SUBLANES):
        shift_ref[r - 1] = wpad_ref[r:r + span, :]
    for c in range(D_CONV // LANES):
        cs = slice(c * LANES, (c + 1) * LANES)
        acc = jnp.zeros((rows, LANES), F32)
        for j in range(CONV_K):
            off = HALO - CONV_PAD + j
            r, base = off % SUBLANES, off - off % SUBLANES
            tap = (wpad_ref[base:base + rows, cs] if r == 0
                   else shift_ref[r - 1, base:base + rows, cs])
            acc = acc + tap * dww_ref[j:j + 1, cs]
        conv_ref[:, cs] = acc + dwb_ref[:, cs]
    y = conv_ref[...]
    mu = jnp.mean(y, axis=-1, keepdims=True)
    yc = y - mu
    var = jnp.mean(yc * yc, axis=-1, keepdims=True)
    y = yc * lax.rsqrt(var + EPS) * lng_ref[...] + lnb_ref[...]
    y = y * jax.nn.sigmoid(y)
    conv_out = jnp.dot(y.astype(BF16), wpw_ref[...], preferred_element_type=F32)

    cn = _rms(conv_out, cog_ref[...]).astype(BF16)
    an = _rms(attn_ref[...].astype(F32), aog_ref[...]).astype(BF16)
    mix = (jnp.dot(cn, wout_ref[0:D_CONV, :], preferred_element_type=F32)
           + jnp.dot(an, wout_ref[D_CONV:, :], preferred_element_type=F32))
    x1 = x_ref[...] + mix
    x1_ref[...] = x1
    h2 = _rms(x1, g2_ref[...])
    hi, lo = _split_bf16(h2)
    h2_ref[...] = hi

    rwh = rwh_ref[...]
    lg = (jnp.dot(hi, rwh, preferred_element_type=F32)
          + jnp.dot(lo, rwh, preferred_element_type=F32)
          + jnp.dot(hi, rwl_ref[...], preferred_element_type=F32)) + rb_ref[...]

    lgt = lg.T[0:N_EXPERTS, :]
    eio = lax.broadcasted_iota(I32, (N_EXPERTS, rows), 0)
    vals, sels = _top4(lgt, eio)
    es = [jnp.exp(v - vals[0]) for v in vals]
    inv = 1.0 / (es[0] + es[1] + es[2] + es[3])
    onehot = jnp.where(sels[0] | sels[1] | sels[2] | sels[3], 1.0, 0.0)
    prefix = jnp.dot(onehot.astype(BF16), tri_ref[...], preferred_element_type=F32)
    n_tok = jnp.sum(onehot, axis=1, keepdims=True)
    n_seg = jnp.maximum(jnp.floor((n_tok + (SEG_ALIGN - 1)) * (1.0 / SEG_ALIGN)), 1.0) * SEG_ALIGN
    off = jnp.dot(low_ref[...], jnp.broadcast_to(n_seg, (N_EXPERTS, LANES)).astype(BF16),
                  preferred_element_type=F32)
    pos = off[:, 0:1] + prefix
    for kk in range(TOP_K):
        gate_ref[kk:kk + 1, :] = es[kk] * inv
        lpos_ref[kk:kk + 1, :] = jnp.sum(jnp.where(sels[kk], pos, 0.0), axis=0,
                                         keepdims=True).astype(I32)
    lane = lax.broadcasted_iota(I32, (N_EXPERTS, LANES), 1)
    seg_ref[...] = jnp.where(lane == 0, n_seg, jnp.where(lane == 1, run_ref[...],
                                                         jnp.where(lane == 2, off, 0.0)))
    run_ref[...] = run_ref[...] + n_seg
    cnt_ref[...] = run_ref[...]


def _post(u, attn, x, p):
    B, S, _ = x.shape
    rows = min(POST_ROWS, S)
    n_i = S // rows
    T = B * S
    hb = rows // HALO
    n_halo = S // HALO
    grid = (B, n_i)
    full = lambda a: pl.BlockSpec(a.shape, lambda b, i: (0,) * a.ndim)
    row_blk = lambda w: pl.BlockSpec((None, rows, w), lambda b, i: (b, i, 0))
    tok_blk = lambda r: pl.BlockSpec((r, rows), lambda b, i: (0, b * n_i + i))
    weights = [p["dw_w"], p["dw_b"], p["ln_g"], p["ln_b"], p["w_pw2"], p["conv_out_g"],
               p["attn_out_g"], p["w_out"], p["norm2_g"], p["rw_hi"], p["rw_lo"], p["rb"],
               p["tri"], p["low"]]
    return pl.pallas_call(
        _post_body,
        grid=grid,
        in_specs=[pl.BlockSpec((None, HALO, D_CONV), lambda b, i: (b, jnp.maximum(i * hb - 1, 0), 0)),
                  row_blk(D_CONV),
                  pl.BlockSpec((None, HALO, D_CONV),
                               lambda b, i: (b, jnp.minimum((i + 1) * hb, n_halo - 1), 0)),
                  row_blk(D_ATTN), row_blk(D_MODEL)] + [full(w) for w in weights],
        out_specs=[row_blk(D_MODEL),
                   pl.BlockSpec((rows, D_MODEL), lambda b, i: (b * n_i + i, 0)),
                   tok_blk(TOP_K), tok_blk(TOP_K),
                   pl.BlockSpec((None, N_EXPERTS, LANES), lambda b, i: (b * n_i + i, 0, 0)),
                   pl.BlockSpec((N_EXPERTS, LANES), lambda b, i: (0, 0))],
        out_shape=[jax.ShapeDtypeStruct((B, S, D_MODEL), F32),
                   jax.ShapeDtypeStruct((T, D_MODEL), BF16),
                   jax.ShapeDtypeStruct((TOP_K, T), I32),
                   jax.ShapeDtypeStruct((TOP_K, T), F32),
                   jax.ShapeDtypeStruct((B * n_i, N_EXPERTS, LANES), F32),
                   jax.ShapeDtypeStruct((N_EXPERTS, LANES), F32)],
        scratch_shapes=[pltpu.VMEM((rows + 2 * HALO, D_CONV), F32),
                        pltpu.VMEM((SUBLANES - 1, rows + 2 * HALO - SUBLANES, D_CONV), F32),
                        pltpu.VMEM((rows, D_CONV), F32),
                        pltpu.VMEM((N_EXPERTS, LANES), F32)],
        compiler_params=_cparams(2),
        name="post",
    )(u, u, u, attn, x, *weights)


def _segments(n_ref, tile, fn, maybe_empty=False):
    for e in range(N_EXPERTS):
        n = n_ref[tile * N_EXPERTS + e]
        if maybe_empty:
            pl.when(n > 0)(functools.partial(fn, e, pl.multiple_of(n, SEG_ALIGN)))
        else:
            fn(e, pl.multiple_of(n, SEG_ALIGN))


def _slot_matrix(lpos_ref, vals, n_slots):
    tile = lpos_ref.shape[1]
    sio = lax.broadcasted_iota(I32, (n_slots, tile), 0)
    m = jnp.zeros((n_slots, tile), F32)
    for kk in range(TOP_K):
        v = 1.0 if vals is None else vals[kk:kk + 1, :]
        m = jnp.where(sio == lpos_ref[kk:kk + 1, :], v, m)
    return m


def _dispatch_body(n_ref, off_ref, dst_ref, tlo_ref, tn_ref,
                   lpos_ref, h2_ref, xs_ref, buf_ref, zero_ref, sem, zsem):
    j = pl.program_id(0)
    n_j = pl.num_programs(0)
    slot = j % 2
    n_slots = buf_ref.shape[1]

    def seg_copy(tile, sl, e, n):
        src = buf_ref.at[sl, pl.ds(pl.multiple_of(off_ref[tile * N_EXPERTS + e], SEG_ALIGN), n), :]
        dst = xs_ref.at[pl.ds(pl.multiple_of(dst_ref[tile * N_EXPERTS + e], SEG_ALIGN), n), :]
        return pltpu.make_async_copy(src, dst, sem.at[sl])

    def tail_copy(e, n):
        return pltpu.make_async_copy(
            zero_ref.at[pl.ds(0, n), :],
            xs_ref.at[pl.ds(pl.multiple_of(tlo_ref[e], SEG_ALIGN), n), :], zsem)

    @pl.when(j == 0)
    def _():
        zero_ref[...] = jnp.zeros_like(zero_ref)
        _segments(tn_ref, 0, lambda e, n: tail_copy(e, n).start(), maybe_empty=True)
        _segments(tn_ref, 0, lambda e, n: tail_copy(e, n).wait(), maybe_empty=True)

    @pl.when(j >= 2)
    def _():
        _segments(n_ref, j - 2, lambda e, n: seg_copy(j - 2, slot, e, n).wait())

    sel = _slot_matrix(lpos_ref, None, n_slots).astype(BF16)
    rows = jnp.dot(sel, h2_ref[...], preferred_element_type=F32)
    buf_ref[slot] = _pack_rows(rows)
    _segments(n_ref, j, lambda e, n: seg_copy(j, slot, e, n).start())

    @pl.when(j == n_j - 1)
    def _():
        @pl.when(j >= 1)
        def _():
            _segments(n_ref, j - 1, lambda e, n: seg_copy(j - 1, 1 - slot, e, n).wait())
        _segments(n_ref, j, lambda e, n: seg_copy(j, slot, e, n).wait())


def _dispatch(seg_n, seg_off, seg_dst, tail_lo, tail_n, lpos, h2, n_rows):
    tile = min(POST_ROWS, h2.shape[0])
    n_t = h2.shape[0] // tile
    n_slots = TOP_K * tile + N_EXPERTS * SEG_ALIGN
    return pl.pallas_call(
        _dispatch_body,
        grid_spec=pltpu.PrefetchScalarGridSpec(
            num_scalar_prefetch=5,
            grid=(n_t,),
            in_specs=[pl.BlockSpec((TOP_K, tile), lambda j, *_: (0, j)),
                      pl.BlockSpec((tile, D_MODEL), lambda j, *_: (j, 0))],
            out_specs=pl.BlockSpec(memory_space=pl.ANY),
            scratch_shapes=[pltpu.VMEM((2, n_slots, D_HALF), U32),
                            pltpu.VMEM((EXPERT_ROWS, D_HALF), U32),
                            pltpu.SemaphoreType.DMA((2,)), pltpu.SemaphoreType.DMA],
        ),
        out_shape=jax.ShapeDtypeStruct((n_rows, D_HALF), U32),
        compiler_params=_cparams(1),
        name="dispatch",
    )(seg_n, seg_off, seg_dst, tail_lo, tail_n, lpos, h2)


def _combine_body(n_ref, off_ref, dst_ref, lpos_ref, gate_ref, x1_ref, fg_ref, ys_ref,
                  y_ref, buf_ref, sem):
    j = pl.program_id(0)
    n_j = pl.num_programs(0)
    slot = j % 2
    n_slots = buf_ref.shape[1]

    def seg_copy(tile, sl, e, n):
        src = ys_ref.at[pl.ds(pl.multiple_of(dst_ref[tile * N_EXPERTS + e], SEG_ALIGN), n), :]
        dst = buf_ref.at[sl, pl.ds(pl.multiple_of(off_ref[tile * N_EXPERTS + e], SEG_ALIGN), n), :]
        return pltpu.make_async_copy(src, dst, sem.at[sl])

    @pl.when(j == 0)
    def _():
        buf_ref[...] = jnp.zeros_like(buf_ref)
        _segments(n_ref, 0, lambda e, n: seg_copy(0, 0, e, n).start())

    @pl.when(j + 1 < n_j)
    def _():
        _segments(n_ref, j + 1, lambda e, n: seg_copy(j + 1, 1 - slot, e, n).start())

    _segments(n_ref, j, lambda e, n: seg_copy(j, slot, e, n).wait())

    w = _slot_matrix(lpos_ref, gate_ref, n_slots).T
    w_hi, w_lo = _split_bf16(w)
    left, right = _unpack_rows(buf_ref[slot])
    moe = jnp.concatenate(
        [jnp.dot(w_hi, half, preferred_element_type=F32) + jnp.dot(w_lo, half, preferred_element_type=F32)
         for half in (left, right)], axis=1)
    y_ref[...] = _rms(x1_ref[...] + moe, fg_ref[...])


def _combine(seg_n, seg_off, seg_dst, lpos, gates, x1, final_g, ys):
    T = x1.shape[0]
    tile = min(POST_ROWS, T)
    n_t = T // tile
    n_slots = TOP_K * tile + N_EXPERTS * SEG_ALIGN
    return pl.pallas_call(
        _combine_body,
        grid_spec=pltpu.PrefetchScalarGridSpec(
            num_scalar_prefetch=3,
            grid=(n_t,),
            in_specs=[pl.BlockSpec((TOP_K, tile), lambda j, *_: (0, j)),
                      pl.BlockSpec((TOP_K, tile), lambda j, *_: (0, j)),
                      pl.BlockSpec((tile, D_MODEL), lambda j, *_: (j, 0)),
                      pl.BlockSpec((1, D_MODEL), lambda j, *_: (0, 0)),
                      pl.BlockSpec(memory_space=pl.ANY)],
            out_specs=pl.BlockSpec((tile, D_MODEL), lambda j, *_: (j, 0)),
            scratch_shapes=[pltpu.VMEM((2, n_slots, D_HALF), U32), pltpu.SemaphoreType.DMA((2,))],
        ),
        out_shape=jax.ShapeDtypeStruct((T, D_MODEL), F32),
        compiler_params=_cparams(1),
        name="combine",
    )(seg_n, seg_off, seg_dst, lpos, gates, x1, final_g, ys)


def _expert_body(be_ref, nact_ref, xs_ref, wgu_ref, bgu_ref, wdn_ref, bdn_ref, ys_ref):
    @pl.when(pl.program_id(0) < nact_ref[0])
    def _():
        left, right = _unpack_rows(xs_ref[...])
        hgu = (jnp.dot(left, wgu_ref[0:D_HALF, :], preferred_element_type=F32)
               + jnp.dot(right, wgu_ref[D_HALF:, :], preferred_element_type=F32)) + bgu_ref[...]
        glu = jnp.minimum(hgu[:, :D_FF], SWIGLU_LIMIT)
        lin = jnp.clip(hgu[:, D_FF:], -SWIGLU_LIMIT, SWIGLU_LIMIT)
        act = glu * jax.nn.sigmoid(SWIGLU_ALPHA * glu) * (lin + 1.0)
        y = jnp.dot(act.astype(BF16), wdn_ref[...], preferred_element_type=F32) + bdn_ref[...]
        ys_ref[...] = _pack_rows(y.astype(BF16).astype(F32))


def _experts(block_expert, n_active, xs, w_gu, b_gu, w_dn, b_dn):
    n_rows = xs.shape[0]
    n_blocks = n_rows // EXPERT_ROWS
    blk = lambda i, be, na: (jnp.minimum(i, na[0] - 1), 0)
    per_e = lambda i, be, na: (be[i], 0, 0)
    return pl.pallas_call(
        _expert_body,
        grid_spec=pltpu.PrefetchScalarGridSpec(
            num_scalar_prefetch=2,
            grid=(n_blocks,),
            in_specs=[pl.BlockSpec((EXPERT_ROWS, D_HALF), blk),
                      pl.BlockSpec((None, D_MODEL, 2 * D_FF), per_e),
                      pl.BlockSpec((None, 1, 2 * D_FF), per_e),
                      pl.BlockSpec((None, D_FF, D_MODEL), per_e),
                      pl.BlockSpec((None, 1, D_MODEL), per_e)],
            out_specs=pl.BlockSpec((EXPERT_ROWS, D_HALF), blk),
        ),
        out_shape=jax.ShapeDtypeStruct((n_rows, D_HALF), U32),
        compiler_params=_cparams(1),
        name="experts",
    )(block_expert, n_active, xs, w_gu, b_gu, w_dn, b_dn)


def _rope_tables(seq_len):
    t = np.arange(seq_len)
    inv = 1.0 / (ROPE_THETA ** (np.arange(0, AXIS_DIM, 2, dtype=np.float32) / AXIS_DIM))
    inv = inv.astype(np.float32)
    ang_r = (t // GRID_W).astype(np.float32)[:, None] * inv[None, :]
    ang_c = (t % GRID_W).astype(np.float32)[:, None] * inv[None, :]
    ang = np.concatenate([ang_r, ang_r, ang_c, ang_c], axis=-1).astype(np.float32)
    cos = np.cos(ang).astype(np.float32)
    sin = np.sin(ang).astype(np.float32)
    sign = np.where((np.arange(HEAD_DIM) % AXIS_DIM) < HALF_AXIS, -1.0, 1.0).astype(np.float32)
    reps = LANES // HEAD_DIM
    return jnp.asarray(np.tile(cos, (1, reps))), jnp.asarray(np.tile(sin * sign, (1, reps)))


def _prepare(norm1_g, w_in, q_norm_g, k_norm_g, conv_dw_w, conv_dw_b, conv_ln_g, conv_ln_b, w_pw2,
             attn_out_g, conv_out_g, w_out, norm2_g, router_w, router_b, w_gate_up, b_gate_up,
             w_down, b_down, final_g):
    row = lambda a: a.reshape(1, -1).astype(F32)
    qk_gain = jnp.concatenate([jnp.tile(q_norm_g[0] * (math.log2(math.e) / math.sqrt(HEAD_DIM)), N_Q_HEADS),
                               jnp.tile(k_norm_g[0], N_KV_HEADS)]).reshape(1, D_QK)
    head = np.arange(LANES) // HEAD_DIM
    hsum = jnp.asarray((head[:, None] == head[None, :]).astype(np.float32), dtype=BF16)
    rw = jnp.pad(router_w[0].astype(F32), ((0, 0), (0, LANES - N_EXPERTS)))
    rw_hi = rw.astype(BF16)
    rw_lo = (rw - rw_hi.astype(F32)).astype(BF16)
    rb = jnp.pad(router_b[0].astype(F32), (0, LANES - N_EXPERTS)).reshape(1, LANES)
    low = np.tril(np.ones((N_EXPERTS, N_EXPERTS), np.float32), k=-1)
    bgu = b_gate_up[0]
    return dict(
        norm1_g=row(norm1_g[0]), w_in=w_in[0].astype(BF16), qk_gain=qk_gain, hsum=hsum,
        dw_w=conv_dw_w[0].astype(F32), dw_b=row(conv_dw_b[0]), ln_g=row(conv_ln_g[0]),
        ln_b=row(conv_ln_b[0]), w_pw2=w_pw2[0].astype(BF16), conv_out_g=row(conv_out_g[0]),
        attn_out_g=row(attn_out_g[0]), w_out=w_out[0].astype(BF16), norm2_g=row(norm2_g[0]),
        rw_hi=rw_hi, rw_lo=rw_lo, rb=rb, low=jnp.asarray(low, dtype=BF16),
        w_gu=_wprep(w_gate_up[0]),
        b_gu=jnp.concatenate([bgu[:, 0::2], bgu[:, 1::2]], axis=-1).reshape(N_EXPERTS, 1, 2 * D_FF),
        w_dn=w_down[0].astype(BF16), b_dn=b_down[0].reshape(N_EXPERTS, 1, D_MODEL),
        final_g=row(final_g),
    )


def _trunk(x, p):
    B, S, _ = x.shape
    T = B * S
    cos, sin = _rope_tables(S)
    q, k, v, u = _pre(x, p["norm1_g"], p["w_in"], p["qk_gain"], cos, sin, p["hsum"])
    attn = _attention(q, k, v)
    tile = min(POST_ROWS, S)
    tri = jnp.asarray(np.triu(np.ones((tile, tile), np.float32), k=1), dtype=BF16)
    x1, h2, lpos, gates, seg, cnt = _post(u, attn, x, dict(p, tri=tri))

    n_t = T // tile
    seg = seg[:, :, 0:3].astype(I32)
    seg_n, seg_base, seg_off = seg[:, :, 0], seg[:, :, 1], seg[:, :, 2]
    counts = cnt[:, 0].astype(I32)
    padded = ((counts + EXPERT_ROWS - 1) // EXPERT_ROWS) * EXPERT_ROWS
    pend = jnp.cumsum(padded)
    pstart = pend - padded
    seg_dst = pstart[None, :] + seg_base
    n_blocks = (T * TOP_K + n_t * N_EXPERTS * SEG_ALIGN) // EXPERT_ROWS + N_EXPERTS
    n_active = pend[-1] // EXPERT_ROWS
    blk_start = jnp.minimum(jnp.arange(n_blocks, dtype=I32), n_active - 1) * EXPERT_ROWS
    block_expert = jnp.sum((blk_start[:, None] >= pend[None, :]).astype(I32), axis=1)
    block_expert = jnp.minimum(block_expert, N_EXPERTS - 1).astype(I32)
    flat = lambda a: a.reshape(-1).astype(I32)

    xs = _dispatch(flat(seg_n), flat(seg_off), flat(seg_dst), flat(pstart + counts),
                   flat(padded - counts), lpos, h2, n_blocks * EXPERT_ROWS)
    ys = _experts(block_expert, n_active.reshape(1).astype(I32), xs,
                  p["w_gu"], p["b_gu"], p["w_dn"], p["b_dn"])
    y = _combine(flat(seg_n), flat(seg_off), flat(seg_dst), lpos, gates,
                 x1.reshape(T, D_MODEL), p["final_g"], ys)
    return y.reshape(B, S, D_MODEL)


def kernel(x_prompt, x_sample, norm1_g, w_in, q_norm_g, k_norm_g, conv_dw_w, conv_dw_b, conv_ln_g,
           conv_ln_b, w_pw2, attn_out_g, conv_out_g, w_out, norm2_g, router_w, router_b, w_gate_up,
           b_gate_up, w_down, b_down, final_g):
    p = _prepare(norm1_g, w_in, q_norm_g, k_norm_g, conv_dw_w, conv_dw_b, conv_ln_g, conv_ln_b,
                 w_pw2, attn_out_g, conv_out_g, w_out, norm2_g, router_w, router_b, w_gate_up,
                 b_gate_up, w_down, b_down, final_g)
    return (_trunk(x_prompt, p), _trunk(x_sample, p))
```

```python
import functools
import math

import numpy as np
import jax
import jax.numpy as jnp
from jax import lax
from jax.experimental import pallas as pl
from jax.experimental.pallas import tpu as pltpu

F32 = jnp.float32
BF16 = jnp.bfloat16
I32 = jnp.int32
U32 = jnp.uint32

D_MODEL = 1024
GRID_W = 64
HEAD_DIM = 64
N_Q_HEADS = 8
N_KV_HEADS = 2
N_GROUPS_PER_KV = N_Q_HEADS // N_KV_HEADS
D_ATTN = N_Q_HEADS * HEAD_DIM
D_KV = N_KV_HEADS * HEAD_DIM
D_QK = D_ATTN + D_KV
AXIS_DIM = HEAD_DIM // 2
HALF_AXIS = AXIS_DIM // 2
ROPE_THETA = 10000.0
D_CONV = D_MODEL - D_ATTN
CONV_K = 31
CONV_PAD = (CONV_K - 1) // 2
D_IN_PROJ = D_ATTN + 2 * D_KV + 2 * D_CONV
N_EXPERTS = 32
TOP_K = 4
D_FF = D_MODEL
SWIGLU_ALPHA = 1.702
SWIGLU_LIMIT = 7.0
EPS = 1e-6

LANES = 128
SUBLANES = 8
HALO = 16
NEG_INF = float("-inf")
D_HALF = D_MODEL // 2
SEG_ALIGN = SUBLANES

PRE_ROWS = 512
ATTN_ROWS = 512
POST_ROWS = 256
EXPERT_ROWS = 512
WPREP_ROWS = 512

VMEM_LIMIT = 56 * 1024 * 1024


def _cparams(n_axes):
    return pltpu.CompilerParams(dimension_semantics=("arbitrary",) * n_axes,
                                vmem_limit_bytes=VMEM_LIMIT)


def _rms(x, g):
    return x * lax.rsqrt(jnp.mean(x * x, axis=-1, keepdims=True) + EPS) * g


def _split_bf16(x):
    hi = x.astype(BF16)
    lo = (x - hi.astype(F32)).astype(BF16)
    return hi, lo


def _pack_rows(x):
    bits = lax.bitcast_convert_type(x, U32)
    return (bits[:, :D_HALF] & jnp.uint32(0xFFFF0000)) | (bits[:, D_HALF:] >> 16)


def _unpack_rows(w):
    left = lax.bitcast_convert_type(w & jnp.uint32(0xFFFF0000), F32).astype(BF16)
    right = lax.bitcast_convert_type(w << 16, F32).astype(BF16)
    return left, right


def _wprep_body(w_ref, perm_ref, o_ref):
    perm = perm_ref[...]
    group = 2 * LANES
    for g in range(2 * D_FF // group):
        blk = w_ref[:, g * group:(g + 1) * group].astype(BF16)
        r = jnp.dot(blk, perm, preferred_element_type=F32)
        o_ref[:, g * LANES:(g + 1) * LANES] = r[:, :LANES].astype(BF16)
        o_ref[:, D_FF + g * LANES:D_FF + (g + 1) * LANES] = r[:, LANES:].astype(BF16)


def _wprep(w_gu):
    group = 2 * LANES
    i = np.arange(group)[:, None]
    c = np.arange(group)[None, :]
    perm = np.where(c < LANES, i == 2 * c, i == 2 * (c - LANES) + 1).astype(np.float32)
    return pl.pallas_call(
        _wprep_body,
        grid=(N_EXPERTS, D_MODEL // WPREP_ROWS),
        in_specs=[pl.BlockSpec((None, WPREP_ROWS, 2 * D_FF), lambda e, r: (e, r, 0)),
                  pl.BlockSpec((group, group), lambda e, r: (0, 0))],
        out_specs=pl.BlockSpec((None, WPREP_ROWS, 2 * D_FF), lambda e, r: (e, r, 0)),
        out_shape=jax.ShapeDtypeStruct((N_EXPERTS, D_MODEL, 2 * D_FF), BF16),
        compiler_params=_cparams(2),
        name="wprep",
    )(w_gu, jnp.asarray(perm, dtype=BF16))


def _pre_body(x_ref, g1_ref, win_ref, qkg_ref, cos_ref, sin_ref, hsum_ref,
              q_ref, k_ref, v_ref, u_ref):
    h = _rms(x_ref[...], g1_ref[...]).astype(BF16)
    z = jnp.dot(h, win_ref[...], preferred_element_type=F32)
    cos = cos_ref[...]
    sin = sin_ref[...]
    lane = lax.broadcasted_iota(I32, (1, LANES), 1)
    first_half = (lane % AXIS_DIM) < HALF_AXIS
    hsum = hsum_ref[...]
    for c in range(D_QK // LANES):
        zc = z[:, c * LANES:(c + 1) * LANES]
        hi, lo = _split_bf16(zc * zc)
        ss = jnp.dot(jnp.concatenate([hi, lo], axis=1), hsum, preferred_element_type=F32)
        zn = zc * lax.rsqrt(ss * (1.0 / HEAD_DIM) + EPS) * qkg_ref[:, c * LANES:(c + 1) * LANES]
        rot = jnp.where(first_half,
                        pltpu.roll(zn, LANES - HALF_AXIS, 1),
                        pltpu.roll(zn, HALF_AXIS, 1))
        out = (zn * cos + rot * sin).astype(BF16)
        if c < D_ATTN // LANES:
            q_ref[:, c * LANES:(c + 1) * LANES] = out
        else:
            k_ref[...] = out
    zv = z[:, D_QK:D_QK + D_KV]
    ones_col = jnp.where(lane == HEAD_DIM, 1.0, 0.0)
    for kv in range(N_KV_HEADS):
        zk = zv if kv == 0 else pltpu.roll(zv, LANES - kv * HEAD_DIM, 1)
        v_ref[:, kv * LANES:(kv + 1) * LANES] = jnp.where(lane < HEAD_DIM, zk, ones_col).astype(BF16)
    a = z[:, D_QK + D_KV:D_QK + D_KV + D_CONV]
    gate = z[:, D_QK + D_KV + D_CONV:]
    u_ref[...] = a * jax.nn.sigmoid(gate)


def _pre(x, g1, w_in, qk_gain, cos, sin, hsum):
    B, S, _ = x.shape
    rows = min(PRE_ROWS, S)
    grid = (B, S // rows)
    row_blk = lambda w: pl.BlockSpec((None, rows, w), lambda b, i: (b, i, 0))
    full = lambda a: pl.BlockSpec(a.shape, lambda b, i: (0,) * a.ndim)
    return pl.pallas_call(
        _pre_body,
        grid=grid,
        in_specs=[row_blk(D_MODEL), full(g1), full(w_in), full(qk_gain),
                  pl.BlockSpec((rows, LANES), lambda b, i: (i, 0)),
                  pl.BlockSpec((rows, LANES), lambda b, i: (i, 0)),
                  full(hsum)],
        out_specs=[row_blk(D_ATTN), row_blk(D_KV), row_blk(N_KV_HEADS * LANES), row_blk(D_CONV)],
        out_shape=[jax.ShapeDtypeStruct((B, S, D_ATTN), BF16),
                   jax.ShapeDtypeStruct((B, S, D_KV), BF16),
                   jax.ShapeDtypeStruct((B, S, N_KV_HEADS * LANES), BF16),
                   jax.ShapeDtypeStruct((B, S, D_CONV), F32)],
        compiler_params=_cparams(2),
        name="pre",
    )(x, g1, w_in, qk_gain, cos, sin, hsum)


def _attn_body(q_ref, k_ref, v_ref, o_ref):
    nt = (((1,), (1,)), ((), ()))
    for kv in range(N_KV_HEADS):
        kh = k_ref[:, kv * HEAD_DIM:(kv + 1) * HEAD_DIM]
        vh = v_ref[:, kv * LANES:(kv + 1) * LANES]
        for g in range(N_GROUPS_PER_KV):
            h = kv * N_GROUPS_PER_KV + g
            qh = q_ref[:, h * HEAD_DIM:(h + 1) * HEAD_DIM]
            s = lax.dot_general(qh, kh, nt, preferred_element_type=F32)
            m = jnp.max(s, axis=-1, keepdims=True)
            p = jnp.exp2(s - m).astype(BF16)
            o = jnp.dot(p, vh, preferred_element_type=F32)
            o_ref[:, h * HEAD_DIM:(h + 1) * HEAD_DIM] = (
                o[:, :HEAD_DIM] / o[:, HEAD_DIM:HEAD_DIM + 1]).astype(BF16)


def _attention(q, k, v):
    B, S, _ = q.shape
    rows = min(ATTN_ROWS, S)
    return pl.pallas_call(
        _attn_body,
        grid=(B, S // rows),
        in_specs=[pl.BlockSpec((None, rows, D_ATTN), lambda b, i: (b, i, 0)),
                  pl.BlockSpec((None, S, D_KV), lambda b, i: (b, 0, 0)),
                  pl.BlockSpec((None, S, N_KV_HEADS * LANES), lambda b, i: (b, 0, 0))],
        out_specs=pl.BlockSpec((None, rows, D_ATTN), lambda b, i: (b, i, 0)),
        out_shape=jax.ShapeDtypeStruct((B, S, D_ATTN), BF16),
        compiler_params=_cparams(2),
        name="attn",
    )(q, k, v)


def _top4(lg, iota):
    vals, sels = [], []
    for _ in range(TOP_K):
        m = jnp.max(lg, axis=0, keepdims=True)
        idx = jnp.min(jnp.where(lg == m, iota, N_EXPERTS), axis=0, keepdims=True)
        sel = iota == idx
        vals.append(m)
        sels.append(sel)
        lg = jnp.where(sel, NEG_INF, lg)
    return vals, sels


def _post_body(up_ref, um_ref, un_ref, attn_ref, x_ref,
               dww_ref, dwb_ref, lng_ref, lnb_ref, wpw_ref, cog_ref, aog_ref, wout_ref,
               g2_ref, rwh_ref, rwl_ref, rb_ref, tri_ref, low_ref,
               x1_ref, h2_ref, lpos_ref, gate_ref, seg_ref, cnt_ref,
               wpad_ref, shift_ref, conv_ref, run_ref):
    b = pl.program_id(0)
    i = pl.program_id(1)
    n_i = pl.num_programs(1)
    rows = um_ref.shape[0]

    @pl.when((b == 0) & (i == 0))
    def _():
        run_ref[...] = jnp.zeros_like(run_ref)

    wpad_ref[0:HALO, :] = jnp.where(i > 0, up_ref[...], 0.0)
    wpad_ref[HALO:HALO + rows, :] = um_ref[...]
    wpad_ref[HALO + rows:, :] = jnp.where(i < n_i - 1, un_ref[...], 0.0)
    span = rows + 2 * HALO - SUBLANES
    for r in range(1, SUBLANES):
        shift_ref[r - 1] = wpad_ref[r:r + span, :]
    for c in range(D_CONV // LANES):
        cs = slice(c * LANES, (c + 1) * LANES)
        acc = jnp.zeros((rows, LANES), F32)
        for j in range(CONV_K):
            off = HALO - CONV_PAD + j
            r, base = off % SUBLANES, off - off % SUBLANES
            tap = (wpad_ref[base:base + rows, cs] if r == 0
                   else shift_ref[r - 1, base:base + rows, cs])
            acc = acc + tap * dww_ref[j:j + 1, cs]
        conv_ref[:, cs] = acc + dwb_ref[:, cs]
    y = conv_ref[...]
    mu = jnp.mean(y, axis=-1, keepdims=True)
    yc = y - mu
    var = jnp.mean(yc * yc, axis=-1, keepdims=True)
    y = yc * lax.rsqrt(var + EPS) * lng_ref[...] + lnb_ref[...]
    y = y * jax.nn.sigmoid(y)
    conv_out = jnp.dot(y.astype(BF16), wpw_ref[...], preferred_element_type=F32)

    cn = _rms(conv_out, cog_ref[...]).astype(BF16)
    an = _rms(attn_ref[...].astype(F32), aog_ref[...]).astype(BF16)
    mix = (jnp.dot(cn, wout_ref[0:D_CONV, :], preferred_element_type=F32)
           + jnp.dot(an, wout_ref[D_CONV:, :], preferred_element_type=F32))
    x1 = x_ref[...] + mix
    x1_ref[...] = x1
    h2 = _rms(x1, g2_ref[...])
    hi, lo = _split_bf16(h2)
    h2_ref[...] = hi

    rwh = rwh_ref[...]
    lg = (jnp.dot(hi, rwh, preferred_element_type=F32)
          + jnp.dot(lo, rwh, preferred_element_type=F32)
          + jnp.dot(hi, rwl_ref[...], preferred_element_type=F32)) + rb_ref[...]

    lgt = lg.T[0:N_EXPERTS, :]
    eio = lax.broadcasted_iota(I32, (N_EXPERTS, rows), 0)
    vals, sels = _top4(lgt, eio)
    es = [jnp.exp(v - vals[0]) for v in vals]
    inv = 1.0 / (es[0] + es[1] + es[2] + es[3])
    onehot = jnp.where(sels[0] | sels[1] | sels[2] | sels[3], 1.0, 0.0)
    prefix = jnp.dot(onehot.astype(BF16), tri_ref[...], preferred_element_type=F32)
    n_tok = jnp.sum(onehot, axis=1, keepdims=True)
    n_seg = jnp.maximum(jnp.floor((n_tok + (SEG_ALIGN - 1)) * (1.0 / SEG_ALIGN)), 1.0) * SEG_ALIGN
    off = jnp.dot(low_ref[...], jnp.broadcast_to(n_seg, (N_EXPERTS, LANES)).astype(BF16),
                  preferred_element_type=F32)
    pos = off[:, 0:1] + prefix
    for kk in range(TOP_K):
        gate_ref[kk:kk + 1, :] = es[kk] * inv
        lpos_ref[kk:kk + 1, :] = jnp.sum(jnp.where(sels[kk], pos, 0.0), axis=0,
                                         keepdims=True).astype(I32)
    lane = lax.broadcasted_iota(I32, (N_EXPERTS, LANES), 1)
    seg_ref[...] = jnp.where(lane == 0, n_seg, jnp.where(lane == 1, run_ref[...],
                                                         jnp.where(lane == 2, off, 0.0)))
    run_ref[...] = run_ref[...] + n_seg
    cnt_ref[...] = run_ref[...]


def _post(u, attn, x, p):
    B, S, _ = x.shape
    rows = min(POST_ROWS, S)
    n_i = S // rows
    T = B * S
    hb = rows // HALO
    n_halo = S // HALO
    grid = (B, n_i)
    full = lambda a: pl.BlockSpec(a.shape, lambda b, i: (0,) * a.ndim)
    row_blk = lambda w: pl.BlockSpec((None, rows, w), lambda b, i: (b, i, 0))
    tok_blk = lambda r: pl.BlockSpec((r, rows), lambda b, i: (0, b * n_i + i))
    weights = [p["dw_w"], p["dw_b"], p["ln_g"], p["ln_b"], p["w_pw2"], p["conv_out_g"],
               p["attn_out_g"], p["w_out"], p["norm2_g"], p["rw_hi"], p["rw_lo"], p["rb"],
               p["tri"], p["low"]]
    return pl.pallas_call(
        _post_body,
        grid=grid,
        in_specs=[pl.BlockSpec((None, HALO, D_CONV), lambda b, i: (b, jnp.maximum(i * hb - 1, 0), 0)),
                  row_blk(D_CONV),
                  pl.BlockSpec((None, HALO, D_CONV),
                               lambda b, i: (b, jnp.minimum((i + 1) * hb, n_halo - 1), 0)),
                  row_blk(D_ATTN), row_blk(D_MODEL)] + [full(w) for w in weights],
        out_specs=[row_blk(D_MODEL),
                   pl.BlockSpec((rows, D_MODEL), lambda b, i: (b * n_i + i, 0)),
                   tok_blk(TOP_K), tok_blk(TOP_K),
                   pl.BlockSpec((None, N_EXPERTS, LANES), lambda b, i: (b * n_i + i, 0, 0)),
                   pl.BlockSpec((N_EXPERTS, LANES), lambda b, i: (0, 0))],
        out_shape=[jax.ShapeDtypeStruct((B, S, D_MODEL), F32),
                   jax.ShapeDtypeStruct((T, D_MODEL), BF16),
                   jax.ShapeDtypeStruct((TOP_K, T), I32),
                   jax.ShapeDtypeStruct((TOP_K, T), F32),
                   jax.ShapeDtypeStruct((B * n_i, N_EXPERTS, LANES), F32),
                   jax.ShapeDtypeStruct((N_EXPERTS, LANES), F32)],
        scratch_shapes=[pltpu.VMEM((rows + 2 * HALO, D_CONV), F32),
                        pltpu.VMEM((SUBLANES - 1, rows + 2 * HALO - SUBLANES, D_CONV), F32),
                        pltpu.VMEM((rows, D_CONV), F32),
                        pltpu.VMEM((N_EXPERTS, LANES), F32)],
        compiler_params=_cparams(2),
        name="post",
    )(u, u, u, attn, x, *weights)


def _segments(n_ref, tile, fn, maybe_empty=False):
    for e in range(N_EXPERTS):
        n = n_ref[tile * N_EXPERTS + e]
        if maybe_empty:
            pl.when(n > 0)(functools.partial(fn, e, pl.multiple_of(n, SEG_ALIGN)))
        else:
            fn(e, pl.multiple_of(n, SEG_ALIGN))


def _slot_matrix(lpos_ref, vals, n_slots):
    tile = lpos_ref.shape[1]
    sio = lax.broadcasted_iota(I32, (n_slots, tile), 0)
    m = jnp.zeros((n_slots, tile), F32)
    for kk in range(TOP_K):
        v = 1.0 if vals is None else vals[kk:kk + 1, :]
        m = jnp.where(sio == lpos_ref[kk:kk + 1, :], v, m)
    return m


def _dispatch_body(n_ref, off_ref, dst_ref, tlo_ref, tn_ref,
                   lpos_ref, h2_ref, xs_ref, buf_ref, zero_ref, sem, zsem):
    j = pl.program_id(0)
    n_j = pl.num_programs(0)
    slot = j % 2
    n_slots = buf_ref.shape[1]

    def seg_copy(tile, sl, e, n):
        src = buf_ref.at[sl, pl.ds(pl.multiple_of(off_ref[tile * N_EXPERTS + e], SEG_ALIGN), n), :]
        dst = xs_ref.at[pl.ds(pl.multiple_of(dst_ref[tile * N_EXPERTS + e], SEG_ALIGN), n), :]
        return pltpu.make_async_copy(src, dst, sem.at[sl])

    def tail_copy(e, n):
        return pltpu.make_async_copy(
            zero_ref.at[pl.ds(0, n), :],
            xs_ref.at[pl.ds(pl.multiple_of(tlo_ref[e], SEG_ALIGN), n), :], zsem)

    @pl.when(j == 0)
    def _():
        zero_ref[...] = jnp.zeros_like(zero_ref)
        _segments(tn_ref, 0, lambda e, n: tail_copy(e, n).start(), maybe_empty=True)
        _segments(tn_ref, 0, lambda e, n: tail_copy(e, n).wait(), maybe_empty=True)

    @pl.when(j >= 2)
    def _():
        _segments(n_ref, j - 2, lambda e, n: seg_copy(j - 2, slot, e, n).wait())

    sel = _slot_matrix(lpos_ref, None, n_slots).astype(BF16)
    rows = jnp.dot(sel, h2_ref[...], preferred_element_type=F32)
    buf_ref[slot] = _pack_rows(rows)
    _segments(n_ref, j, lambda e, n: seg_copy(j, slot, e, n).start())

    @pl.when(j == n_j - 1)
    def _():
        @pl.when(j >= 1)
        def _():
            _segments(n_ref, j - 1, lambda e, n: seg_copy(j - 1, 1 - slot, e, n).wait())
        _segments(n_ref, j, lambda e, n: seg_copy(j, slot, e, n).wait())


def _dispatch(seg_n, seg_off, seg_dst, tail_lo, tail_n, lpos, h2, n_rows):
    tile = min(POST_ROWS, h2.shape[0])
    n_t = h2.shape[0] // tile
    n_slots = TOP_K * tile + N_EXPERTS * SEG_ALIGN
    return pl.pallas_call(
        _dispatch_body,
        grid_spec=pltpu.PrefetchScalarGridSpec(
            num_scalar_prefetch=5,
            grid=(n_t,),
            in_specs=[pl.BlockSpec((TOP_K, tile), lambda j, *_: (0, j)),
                      pl.BlockSpec((tile, D_MODEL), lambda j, *_: (j, 0))],
            out_specs=pl.BlockSpec(memory_space=pl.ANY),
            scratch_shapes=[pltpu.VMEM((2, n_slots, D_HALF), U32),
                            pltpu.VMEM((EXPERT_ROWS, D_HALF), U32),
                            pltpu.SemaphoreType.DMA((2,)), pltpu.SemaphoreType.DMA],
        ),
        out_shape=jax.ShapeDtypeStruct((n_rows, D_HALF), U32),
        compiler_params=_cparams(1),
        name="dispatch",
    )(seg_n, seg_off, seg_dst, tail_lo, tail_n, lpos, h2)


def _combine_body(n_ref, off_ref, dst_ref, lpos_ref, gate_ref, x1_ref, fg_ref, ys_ref,
                  y_ref, buf_ref, sem):
    j = pl.program_id(0)
    n_j = pl.num_programs(0)
    slot = j % 2
    n_slots = buf_ref.shape[1]

    def seg_copy(tile, sl, e, n):
        src = ys_ref.at[pl.ds(pl.multiple_of(dst_ref[tile * N_EXPERTS + e], SEG_ALIGN), n), :]
        dst = buf_ref.at[sl, pl.ds(pl.multiple_of(off_ref[tile * N_EXPERTS + e], SEG_ALIGN), n), :]
        return pltpu.make_async_copy(src, dst, sem.at[sl])

    @pl.when(j == 0)
    def _():
        buf_ref[...] = jnp.zeros_like(buf_ref)
        _segments(n_ref, 0, lambda e, n: seg_copy(0, 0, e, n).start())

    @pl.when(j + 1 < n_j)
    def _():
        _segments(n_ref, j + 1, lambda e, n: seg_copy(j + 1, 1 - slot, e, n).start())

    _segments(n_ref, j, lambda e, n: seg_copy(j, slot, e, n).wait())

    w = _slot_matrix(lpos_ref, gate_ref, n_slots).T.astype(BF16)
    left, right = _unpack_rows(buf_ref[slot])
    moe = jnp.concatenate(
        [jnp.dot(w, half, preferred_element_type=F32) for half in (left, right)], axis=1)
    y_ref[...] = _rms(x1_ref[...] + moe, fg_ref[...])


def _combine(seg_n, seg_off, seg_dst, lpos, gates, x1, final_g, ys):
    T = x1.shape[0]
    tile = min(POST_ROWS, T)
    n_t = T // tile
    n_slots = TOP_K * tile + N_EXPERTS * SEG_ALIGN
    return pl.pallas_call(
        _combine_body,
        grid_spec=pltpu.PrefetchScalarGridSpec(
            num_scalar_prefetch=3,
            grid=(n_t,),
            in_specs=[pl.BlockSpec((TOP_K, tile), lambda j, *_: (0, j)),
                      pl.BlockSpec((TOP_K, tile), lambda j, *_: (0, j)),
                      pl.BlockSpec((tile, D_MODEL), lambda j, *_: (j, 0)),
                      pl.BlockSpec((1, D_MODEL), lambda j, *_: (0, 0)),
                      pl.BlockSpec(memory_space=pl.ANY)],
            out_specs=pl.BlockSpec((tile, D_MODEL), lambda j, *_: (j, 0)),
            scratch_shapes=[pltpu.VMEM((2, n_slots, D_HALF), U32), pltpu.SemaphoreType.DMA((2,))],
        ),
        out_shape=jax.ShapeDtypeStruct((T, D_MODEL), F32),
        compiler_params=_cparams(1),
        name="combine",
    )(seg_n, seg_off, seg_dst, lpos, gates, x1, final_g, ys)


def _expert_body(be_ref, nact_ref, xs_ref, wgu_ref, bgu_ref, wdn_ref, bdn_ref, ys_ref):
    @pl.when(pl.program_id(0) < nact_ref[0])
    def _():
        left, right = _unpack_rows(xs_ref[...])
        hgu = (jnp.dot(left, wgu_ref[0:D_HALF, :], preferred_element_type=F32)
               + jnp.dot(right, wgu_ref[D_HALF:, :], preferred_element_type=F32)) + bgu_ref[...]
        glu = jnp.minimum(hgu[:, :D_FF], SWIGLU_LIMIT)
        lin = jnp.clip(hgu[:, D_FF:], -SWIGLU_LIMIT, SWIGLU_LIMIT)
        act = glu * jax.nn.sigmoid(SWIGLU_ALPHA * glu) * (lin + 1.0)
        y = jnp.dot(act.astype(BF16), wdn_ref[...], preferred_element_type=F32) + bdn_ref[...]
        ys_ref[...] = _pack_rows(y.astype(BF16).astype(F32))


def _experts(block_expert, n_active, xs, w_gu, b_gu, w_dn, b_dn):
    n_rows = xs.shape[0]
    n_blocks = n_rows // EXPERT_ROWS
    blk = lambda i, be, na: (jnp.minimum(i, na[0] - 1), 0)
    per_e = lambda i, be, na: (be[i], 0, 0)
    return pl.pallas_call(
        _expert_body,
        grid_spec=pltpu.PrefetchScalarGridSpec(
            num_scalar_prefetch=2,
            grid=(n_blocks,),
            in_specs=[pl.BlockSpec((EXPERT_ROWS, D_HALF), blk),
                      pl.BlockSpec((None, D_MODEL, 2 * D_FF), per_e),
                      pl.BlockSpec((None, 1, 2 * D_FF), per_e),
                      pl.BlockSpec((None, D_FF, D_MODEL), per_e),
                      pl.BlockSpec((None, 1, D_MODEL), per_e)],
            out_specs=pl.BlockSpec((EXPERT_ROWS, D_HALF), blk),
        ),
        out_shape=jax.ShapeDtypeStruct((n_rows, D_HALF), U32),
        compiler_params=_cparams(1),
        name="experts",
    )(block_expert, n_active, xs, w_gu, b_gu, w_dn, b_dn)


def _rope_tables(seq_len):
    t = np.arange(seq_len)
    inv = 1.0 / (ROPE_THETA ** (np.arange(0, AXIS_DIM, 2, dtype=np.float32) / AXIS_DIM))
    inv = inv.astype(np.float32)
    ang_r = (t // GRID_W).astype(np.float32)[:, None] * inv[None, :]
    ang_c = (t % GRID_W).astype(np.float32)[:, None] * inv[None, :]
    ang = np.concatenate([ang_r, ang_r, ang_c, ang_c], axis=-1).astype(np.float32)
    cos = np.cos(ang).astype(np.float32)
    sin = np.sin(ang).astype(np.float32)
    sign = np.where((np.arange(HEAD_DIM) % AXIS_DIM) < HALF_AXIS, -1.0, 1.0).astype(np.float32)
    reps = LANES // HEAD_DIM
    return jnp.asarray(np.tile(cos, (1, reps))), jnp.asarray(np.tile(sin * sign, (1, reps)))


def _prepare(norm1_g, w_in, q_norm_g, k_norm_g, conv_dw_w, conv_dw_b, conv_ln_g, conv_ln_b, w_pw2,
             attn_out_g, conv_out_g, w_out, norm2_g, router_w, router_b, w_gate_up, b_gate_up,
             w_down, b_down, final_g):
    row = lambda a: a.reshape(1, -1).astype(F32)
    qk_gain = jnp.concatenate([jnp.tile(q_norm_g[0] * (math.log2(math.e) / math.sqrt(HEAD_DIM)), N_Q_HEADS),
                               jnp.tile(k_norm_g[0], N_KV_HEADS)]).reshape(1, D_QK)
    head = np.arange(LANES) // HEAD_DIM
    hsum = jnp.asarray(np.tile((head[:, None] == head[None, :]).astype(np.float32), (2, 1)), dtype=BF16)
    rw = jnp.pad(router_w[0].astype(F32), ((0, 0), (0, LANES - N_EXPERTS)))
    rw_hi = rw.astype(BF16)
    rw_lo = (rw - rw_hi.astype(F32)).astype(BF16)
    rb = jnp.pad(router_b[0].astype(F32), (0, LANES - N_EXPERTS)).reshape(1, LANES)
    low = np.tril(np.ones((N_EXPERTS, N_EXPERTS), np.float32), k=-1)
    bgu = b_gate_up[0]
    return dict(
        norm1_g=row(norm1_g[0]), w_in=w_in[0].astype(BF16), qk_gain=qk_gain, hsum=hsum,
        dw_w=conv_dw_w[0].astype(F32), dw_b=row(conv_dw_b[0]), ln_g=row(conv_ln_g[0]),
        ln_b=row(conv_ln_b[0]), w_pw2=w_pw2[0].astype(BF16), conv_out_g=row(conv_out_g[0]),
        attn_out_g=row(attn_out_g[0]), w_out=w_out[0].astype(BF16), norm2_g=row(norm2_g[0]),
        rw_hi=rw_hi, rw_lo=rw_lo, rb=rb, low=jnp.asarray(low, dtype=BF16),
        w_gu=_wprep(w_gate_up[0]),
        b_gu=jnp.concatenate([bgu[:, 0::2], bgu[:, 1::2]], axis=-1).reshape(N_EXPERTS, 1, 2 * D_FF),
        w_dn=w_down[0].astype(BF16), b_dn=b_down[0].reshape(N_EXPERTS, 1, D_MODEL),
        final_g=row(final_g),
    )


def _trunk(x, p):
    B, S, _ = x.shape
    T = B * S
    cos, sin = _rope_tables(S)
    q, k, v, u = _pre(x, p["norm1_g"], p["w_in"], p["qk_gain"], cos, sin, p["hsum"])
    attn = _attention(q, k, v)
    tile = min(POST_ROWS, S)
    tri = jnp.asarray(np.triu(np.ones((tile, tile), np.float32), k=1), dtype=BF16)
    x1, h2, lpos, gates, seg, cnt = _post(u, attn, x, dict(p, tri=tri))

    n_t = T // tile
    seg = seg[:, :, 0:3].astype(I32)
    seg_n, seg_base, seg_off = seg[:, :, 0], seg[:, :, 1], seg[:, :, 2]
    counts = cnt[:, 0].astype(I32)
    padded = ((counts + EXPERT_ROWS - 1) // EXPERT_ROWS) * EXPERT_ROWS
    pend = jnp.cumsum(padded)
    pstart = pend - padded
    seg_dst = pstart[None, :] + seg_base
    n_blocks = (T * TOP_K + n_t * N_EXPERTS * SEG_ALIGN) // EXPERT_ROWS + N_EXPERTS
    n_active = pend[-1] // EXPERT_ROWS
    blk_start = jnp.minimum(jnp.arange(n_blocks, dtype=I32), n_active - 1) * EXPERT_ROWS
    block_expert = jnp.sum((blk_start[:, None] >= pend[None, :]).astype(I32), axis=1)
    block_expert = jnp.minimum(block_expert, N_EXPERTS - 1).astype(I32)
    flat = lambda a: a.reshape(-1).astype(I32)

    xs = _dispatch(flat(seg_n), flat(seg_off), flat(seg_dst), flat(pstart + counts),
                   flat(padded - counts), lpos, h2, n_blocks * EXPERT_ROWS)
    ys = _experts(block_expert, n_active.reshape(1).astype(I32), xs,
                  p["w_gu"], p["b_gu"], p["w_dn"], p["b_dn"])
    y = _combine(flat(seg_n), flat(seg_off), flat(seg_dst), lpos, gates,
                 x1.reshape(T, D_MODEL), p["final_g"], ys)
    return y.reshape(B, S, D_MODEL)


def kernel(x_prompt, x_sample, norm1_g, w_in, q_norm_g, k_norm_g, conv_dw_w, conv_dw_b, conv_ln_g,
           conv_ln_b, w_pw2, attn_out_g, conv_out_g, w_out, norm2_g, router_w, router_b, w_gate_up,
           b_gate_up, w_down, b_down, final_g):
    p = _prepare(norm1_g, w_in, q_norm_g, k_norm_g, conv_dw_w, conv_dw_b, conv_ln_g, conv_ln_b,
                 w_pw2, attn_out_g, conv_out_g, w_out, norm2_g, router_w, router_b, w_gate_up,
                 b_gate_up, w_down, b_down, final_g)
    return (_trunk(x_prompt, p), _trunk(x_sample, p))
```

```python
import functools
import math

import numpy as np
import jax
import jax.numpy as jnp
from jax import lax
from jax.experimental import pallas as pl
from jax.experimental.pallas import tpu as pltpu

F32 = jnp.float32
BF16 = jnp.bfloat16
I32 = jnp.int32
U32 = jnp.uint32

D_MODEL = 1024
GRID_W = 64
HEAD_DIM = 64
N_Q_HEADS = 8
N_KV_HEADS = 2
N_GROUPS_PER_KV = N_Q_HEADS // N_KV_HEADS
D_ATTN = N_Q_HEADS * HEAD_DIM
D_KV = N_KV_HEADS * HEAD_DIM
D_QK = D_ATTN + D_KV
AXIS_DIM = HEAD_DIM // 2
HALF_AXIS = AXIS_DIM // 2
ROPE_THETA = 10000.0
D_CONV = D_MODEL - D_ATTN
CONV_K = 31
CONV_PAD = (CONV_K - 1) // 2
D_IN_PROJ = D_ATTN + 2 * D_KV + 2 * D_CONV
N_EXPERTS = 32
TOP_K = 4
D_FF = D_MODEL
SWIGLU_ALPHA = 1.702
SWIGLU_LIMIT = 7.0
EPS = 1e-6

LANES = 128
SUBLANES = 8
HALO = 16
NEG_INF = float("-inf")
D_HALF = D_MODEL // 2
SEG_ALIGN = SUBLANES

PRE_ROWS = 512
MIX_ROWS = 512
CONV_ROWS = 256
ROUTE_ROWS = 256
EXPERT_ROWS = 512
WPREP_ROWS = 512
FF_CHUNK = D_FF

VMEM_LIMIT = 56 * 1024 * 1024


def _cparams(n_axes):
    return pltpu.CompilerParams(dimension_semantics=("arbitrary",) * n_axes,
                                vmem_limit_bytes=VMEM_LIMIT)


def _rms(x, g):
    return x * lax.rsqrt(jnp.mean(x * x, axis=-1, keepdims=True) + EPS) * g


def _split_bf16(x):
    hi = x.astype(BF16)
    lo = (x - hi.astype(F32)).astype(BF16)
    return hi, lo


def _pack_rows(x):
    bits = lax.bitcast_convert_type(x, U32)
    return (bits[:, :D_HALF] & jnp.uint32(0xFFFF0000)) | (bits[:, D_HALF:] >> 16)


def _unpack_rows(w):
    left = lax.bitcast_convert_type(w & jnp.uint32(0xFFFF0000), F32).astype(BF16)
    right = lax.bitcast_convert_type(w << 16, F32).astype(BF16)
    return left, right


def _wprep_body(w_ref, perm_ref, o_ref):
    perm = perm_ref[...]
    group = 2 * LANES
    for g in range(2 * D_FF // group):
        blk = w_ref[:, g * group:(g + 1) * group].astype(BF16)
        r = jnp.dot(blk, perm, preferred_element_type=F32)
        c, half = divmod(g * LANES, FF_CHUNK)
        o_ref[:, 2 * c * FF_CHUNK + half:2 * c * FF_CHUNK + half + LANES] = r[:, :LANES].astype(BF16)
        o_ref[:, (2 * c + 1) * FF_CHUNK + half:(2 * c + 1) * FF_CHUNK + half + LANES] = (
            r[:, LANES:].astype(BF16))


def _wprep(w_gu):
    group = 2 * LANES
    i = np.arange(group)[:, None]
    c = np.arange(group)[None, :]
    perm = np.where(c < LANES, i == 2 * c, i == 2 * (c - LANES) + 1).astype(np.float32)
    return pl.pallas_call(
        _wprep_body,
        grid=(N_EXPERTS, D_MODEL // WPREP_ROWS),
        in_specs=[pl.BlockSpec((None, WPREP_ROWS, 2 * D_FF), lambda e, r: (e, r, 0)),
                  pl.BlockSpec((group, group), lambda e, r: (0, 0))],
        out_specs=pl.BlockSpec((None, WPREP_ROWS, 2 * D_FF), lambda e, r: (e, r, 0)),
        out_shape=jax.ShapeDtypeStruct((N_EXPERTS, D_MODEL, 2 * D_FF), BF16),
        compiler_params=_cparams(2),
        name="wprep",
    )(w_gu, jnp.asarray(perm, dtype=BF16))


def _pre_body(x_ref, g1_ref, win_ref, qkg_ref, cos_ref, sin_ref, hsum_ref,
              q_ref, k_ref, v_ref, u_ref):
    h = _rms(x_ref[...], g1_ref[...]).astype(BF16)
    z = jnp.dot(h, win_ref[...], preferred_element_type=F32)
    cos = cos_ref[...]
    sin = sin_ref[...]
    lane = lax.broadcasted_iota(I32, (1, LANES), 1)
    first_half = (lane % AXIS_DIM) < HALF_AXIS
    hsum = hsum_ref[...]
    for c in range(D_QK // LANES):
        zc = z[:, c * LANES:(c + 1) * LANES]
        hi, lo = _split_bf16(zc * zc)
        ss = jnp.dot(jnp.concatenate([hi, lo], axis=1), hsum, preferred_element_type=F32)
        zn = zc * lax.rsqrt(ss * (1.0 / HEAD_DIM) + EPS) * qkg_ref[:, c * LANES:(c + 1) * LANES]
        rot = jnp.where(first_half,
                        pltpu.roll(zn, LANES - HALF_AXIS, 1),
                        pltpu.roll(zn, HALF_AXIS, 1))
        out = (zn * cos + rot * sin).astype(BF16)
        if c < D_ATTN // LANES:
            q_ref[:, c * LANES:(c + 1) * LANES] = out
        else:
            k_ref[...] = out
    zv = z[:, D_QK:D_QK + D_KV]
    ones_col = jnp.where(lane == HEAD_DIM, 1.0, 0.0)
    for kv in range(N_KV_HEADS):
        zk = zv if kv == 0 else pltpu.roll(zv, LANES - kv * HEAD_DIM, 1)
        v_ref[:, kv * LANES:(kv + 1) * LANES] = jnp.where(lane < HEAD_DIM, zk, ones_col).astype(BF16)
    a = z[:, D_QK + D_KV:D_QK + D_KV + D_CONV]
    gate = z[:, D_QK + D_KV + D_CONV:]
    u_ref[...] = a * jax.nn.sigmoid(gate)


def _pre(x, g1, w_in, qk_gain, cos, sin, hsum):
    B, S, _ = x.shape
    rows = min(PRE_ROWS, S)
    grid = (B, S // rows)
    row_blk = lambda w: pl.BlockSpec((None, rows, w), lambda b, i: (b, i, 0))
    full = lambda a: pl.BlockSpec(a.shape, lambda b, i: (0,) * a.ndim)
    return pl.pallas_call(
        _pre_body,
        grid=grid,
        in_specs=[row_blk(D_MODEL), full(g1), full(w_in), full(qk_gain),
                  pl.BlockSpec((rows, LANES), lambda b, i: (i, 0)),
                  pl.BlockSpec((rows, LANES), lambda b, i: (i, 0)),
                  full(hsum)],
        out_specs=[row_blk(D_ATTN), row_blk(D_KV), row_blk(N_KV_HEADS * LANES), row_blk(D_CONV)],
        out_shape=[jax.ShapeDtypeStruct((B, S, D_ATTN), BF16),
                   jax.ShapeDtypeStruct((B, S, D_KV), BF16),
                   jax.ShapeDtypeStruct((B, S, N_KV_HEADS * LANES), BF16),
                   jax.ShapeDtypeStruct((B, S, D_CONV), F32)],
        compiler_params=_cparams(2),
        name="pre",
    )(x, g1, w_in, qk_gain, cos, sin, hsum)


def _attend(q_ref, k_ref, v_ref, attn_ref):
    nt = (((1,), (1,)), ((), ()))
    for kv in range(N_KV_HEADS):
        kh = k_ref[:, kv * HEAD_DIM:(kv + 1) * HEAD_DIM]
        vh = v_ref[:, kv * LANES:(kv + 1) * LANES]
        for g in range(N_GROUPS_PER_KV):
            h = kv * N_GROUPS_PER_KV + g
            qh = q_ref[:, h * HEAD_DIM:(h + 1) * HEAD_DIM]
            s = lax.dot_general(qh, kh, nt, preferred_element_type=F32)
            m = jnp.max(s, axis=-1, keepdims=True)
            p = jnp.exp2(s - m).astype(BF16)
            o = jnp.dot(p, vh, preferred_element_type=F32)
            attn_ref[:, h * HEAD_DIM:(h + 1) * HEAD_DIM] = (
                o[:, :HEAD_DIM] / o[:, HEAD_DIM:HEAD_DIM + 1]).astype(BF16)


def _top4(lg, iota):
    vals, sels = [], []
    for _ in range(TOP_K):
        m = jnp.max(lg, axis=0, keepdims=True)
        idx = jnp.min(jnp.where(lg == m, iota, N_EXPERTS), axis=0, keepdims=True)
        sel = iota == idx
        vals.append(m)
        sels.append(sel)
        lg = jnp.where(sel, NEG_INF, lg)
    return vals, sels


def _route(lg, tri_ref, low_ref, run_ref, lpos_ref, gate_ref, seg_ref, cols):
    tile = lg.shape[0]
    lgt = lg.T[0:N_EXPERTS, :]
    eio = lax.broadcasted_iota(I32, (N_EXPERTS, tile), 0)
    vals, sels = _top4(lgt, eio)
    es = [jnp.exp(v - vals[0]) for v in vals]
    inv = 1.0 / (es[0] + es[1] + es[2] + es[3])
    onehot = jnp.where(sels[0] | sels[1] | sels[2] | sels[3], 1.0, 0.0)
    prefix = jnp.dot(onehot.astype(BF16), tri_ref[...], preferred_element_type=F32)
    n_tok = jnp.sum(onehot, axis=1, keepdims=True)
    n_seg = jnp.maximum(jnp.floor((n_tok + (SEG_ALIGN - 1)) * (1.0 / SEG_ALIGN)), 1.0) * SEG_ALIGN
    off = jnp.dot(low_ref[...], jnp.broadcast_to(n_seg, (N_EXPERTS, LANES)).astype(BF16),
                  preferred_element_type=F32)
    pos = off[:, 0:1] + prefix
    for kk in range(TOP_K):
        gate_ref[kk:kk + 1, cols] = es[kk] * inv
        lpos_ref[kk:kk + 1, cols] = jnp.sum(jnp.where(sels[kk], pos, 0.0), axis=0,
                                            keepdims=True).astype(I32)
    lane = lax.broadcasted_iota(I32, (N_EXPERTS, LANES), 1)
    seg_ref[...] = jnp.where(lane == 0, n_seg, jnp.where(lane == 1, run_ref[...],
                                                         jnp.where(lane == 2, off, 0.0)))
    run_ref[...] = run_ref[...] + n_seg


def _mix_body(q_ref, k_ref, v_ref, up_ref, um_ref, un_ref, x_ref,
              dww_ref, dwb_ref, lng_ref, lnb_ref, wpw_ref, cog_ref, aog_ref, wout_ref,
              g2_ref, rwh_ref, rwl_ref, rb_ref, tri_ref, low_ref,
              x1_ref, h2_ref, lpos_ref, gate_ref, seg_ref, cnt_ref,
              attn_ref, wpad_ref, shift_ref, conv_ref, run_ref):
    b = pl.program_id(0)
    i = pl.program_id(1)
    n_i = pl.num_programs(1)
    rows = um_ref.shape[0]
    tile = tri_ref.shape[0]

    @pl.when((b == 0) & (i == 0))
    def _():
        run_ref[...] = jnp.zeros_like(run_ref)

    wpad_ref[0:HALO, :] = jnp.where(i > 0, up_ref[...], 0.0)
    wpad_ref[HALO:HALO + rows, :] = um_ref[...]
    wpad_ref[HALO + rows:, :] = jnp.where(i < n_i - 1, un_ref[...], 0.0)
    span = rows + 2 * HALO - SUBLANES
    for r in range(1, SUBLANES):
        shift_ref[r - 1] = wpad_ref[r:r + span, :]
    part = min(CONV_ROWS, rows)
    for c in range(D_CONV // LANES):
        cs = slice(c * LANES, (c + 1) * LANES)
        for r0 in range(0, rows, part):
            acc = jnp.zeros((part, LANES), F32)
            for j in range(CONV_K):
                off = HALO - CONV_PAD + j
                r, base = off % SUBLANES, r0 + off - off % SUBLANES
                tap = (wpad_ref[base:base + part, cs] if r == 0
                       else shift_ref[r - 1, base:base + part, cs])
                acc = acc + tap * dww_ref[j:j + 1, cs]
            conv_ref[r0:r0 + part, cs] = acc + dwb_ref[:, cs]
    _attend(q_ref, k_ref, v_ref, attn_ref)
    y = conv_ref[...]
    mu = jnp.mean(y, axis=-1, keepdims=True)
    yc = y - mu
    var = jnp.mean(yc * yc, axis=-1, keepdims=True)
    y = yc * lax.rsqrt(var + EPS) * lng_ref[...] + lnb_ref[...]
    y = y * jax.nn.sigmoid(y)
    conv_out = jnp.dot(y.astype(BF16), wpw_ref[...], preferred_element_type=F32)

    cn = _rms(conv_out, cog_ref[...]).astype(BF16)
    an = _rms(attn_ref[...].astype(F32), aog_ref[...]).astype(BF16)
    mix = (jnp.dot(cn, wout_ref[0:D_CONV, :], preferred_element_type=F32)
           + jnp.dot(an, wout_ref[D_CONV:, :], preferred_element_type=F32))
    x1 = x_ref[...] + mix
    x1_ref[...] = x1
    h2 = _rms(x1, g2_ref[...])
    hi, lo = _split_bf16(h2)
    h2_ref[...] = hi

    rwh = rwh_ref[...]
    lg = (jnp.dot(hi, rwh, preferred_element_type=F32)
          + jnp.dot(lo, rwh, preferred_element_type=F32)
          + jnp.dot(hi, rwl_ref[...], preferred_element_type=F32)) + rb_ref[...]
    for t in range(rows // tile):
        cols = slice(t * tile, (t + 1) * tile)
        _route(lg[cols, :], tri_ref, low_ref, run_ref, lpos_ref, gate_ref, seg_ref.at[t], cols)
    cnt_ref[...] = run_ref[...]


def _mix(q, k, v, u, x, p):
    B, S, _ = x.shape
    rows = min(MIX_ROWS, S)
    tile = p["tri"].shape[0]
    n_i = S // rows
    T = B * S
    hb = rows // HALO
    n_halo = S // HALO
    full = lambda a: pl.BlockSpec(a.shape, lambda b, i: (0,) * a.ndim)
    row_blk = lambda w: pl.BlockSpec((None, rows, w), lambda b, i: (b, i, 0))
    seq_blk = lambda w: pl.BlockSpec((None, S, w), lambda b, i: (b, 0, 0))
    tok_blk = lambda r: pl.BlockSpec((r, rows), lambda b, i: (0, b * n_i + i))
    weights = [p["dw_w"], p["dw_b"], p["ln_g"], p["ln_b"], p["w_pw2"], p["conv_out_g"],
               p["attn_out_g"], p["w_out"], p["norm2_g"], p["rw_hi"], p["rw_lo"], p["rb"],
               p["tri"], p["low"]]
    return pl.pallas_call(
        _mix_body,
        grid=(B, n_i),
        in_specs=[row_blk(D_ATTN), seq_blk(D_KV), seq_blk(N_KV_HEADS * LANES),
                  pl.BlockSpec((None, HALO, D_CONV), lambda b, i: (b, jnp.maximum(i * hb - 1, 0), 0)),
                  row_blk(D_CONV),
                  pl.BlockSpec((None, HALO, D_CONV),
                               lambda b, i: (b, jnp.minimum((i + 1) * hb, n_halo - 1), 0)),
                  row_blk(D_MODEL)] + [full(w) for w in weights],
        out_specs=[row_blk(D_MODEL),
                   pl.BlockSpec((rows, D_MODEL), lambda b, i: (b * n_i + i, 0)),
                   tok_blk(TOP_K), tok_blk(TOP_K),
                   pl.BlockSpec((rows // tile, N_EXPERTS, LANES), lambda b, i: (b * n_i + i, 0, 0)),
                   pl.BlockSpec((N_EXPERTS, LANES), lambda b, i: (0, 0))],
        out_shape=[jax.ShapeDtypeStruct((B, S, D_MODEL), F32),
                   jax.ShapeDtypeStruct((T, D_MODEL), BF16),
                   jax.ShapeDtypeStruct((TOP_K, T), I32),
                   jax.ShapeDtypeStruct((TOP_K, T), F32),
                   jax.ShapeDtypeStruct((T // tile, N_EXPERTS, LANES), F32),
                   jax.ShapeDtypeStruct((N_EXPERTS, LANES), F32)],
        scratch_shapes=[pltpu.VMEM((rows, D_ATTN), BF16),
                        pltpu.VMEM((rows + 2 * HALO, D_CONV), F32),
                        pltpu.VMEM((SUBLANES - 1, rows + 2 * HALO - SUBLANES, D_CONV), F32),
                        pltpu.VMEM((rows, D_CONV), F32),
                        pltpu.VMEM((N_EXPERTS, LANES), F32)],
        compiler_params=_cparams(2),
        name="mix",
    )(q, k, v, u, u, u, x, *weights)


def _segments(n_ref, tile, fn, maybe_empty=False):
    for e in range(N_EXPERTS):
        n = n_ref[tile * N_EXPERTS + e]
        if maybe_empty:
            pl.when(n > 0)(functools.partial(fn, e, pl.multiple_of(n, SEG_ALIGN)))
        else:
            fn(e, pl.multiple_of(n, SEG_ALIGN))


def _slot_matrix(lpos_ref, vals, n_slots):
    tile = lpos_ref.shape[1]
    sio = lax.broadcasted_iota(I32, (n_slots, tile), 0)
    m = jnp.zeros((n_slots, tile), F32)
    for kk in range(TOP_K):
        v = 1.0 if vals is None else vals[kk:kk + 1, :]
        m = jnp.where(sio == lpos_ref[kk:kk + 1, :], v, m)
    return m


def _dispatch_body(n_ref, off_ref, dst_ref, tlo_ref, tn_ref,
                   lpos_ref, h2_ref, xs_ref, buf_ref, zero_ref, sem, zsem):
    j = pl.program_id(0)
    n_j = pl.num_programs(0)
    slot = j % 2
    n_slots = buf_ref.shape[1]

    def seg_copy(tile, sl, e, n):
        src = buf_ref.at[sl, pl.ds(pl.multiple_of(off_ref[tile * N_EXPERTS + e], SEG_ALIGN), n), :]
        dst = xs_ref.at[pl.ds(pl.multiple_of(dst_ref[tile * N_EXPERTS + e], SEG_ALIGN), n), :]
        return pltpu.make_async_copy(src, dst, sem.at[sl])

    def tail_copy(e, n):
        return pltpu.make_async_copy(
            zero_ref.at[pl.ds(0, n), :],
            xs_ref.at[pl.ds(pl.multiple_of(tlo_ref[e], SEG_ALIGN), n), :], zsem)

    @pl.when(j == 0)
    def _():
        zero_ref[...] = jnp.zeros_like(zero_ref)
        _segments(tn_ref, 0, lambda e, n: tail_copy(e, n).start(), maybe_empty=True)
        _segments(tn_ref, 0, lambda e, n: tail_copy(e, n).wait(), maybe_empty=True)

    @pl.when(j >= 2)
    def _():
        _segments(n_ref, j - 2, lambda e, n: seg_copy(j - 2, slot, e, n).wait())

    sel = _slot_matrix(lpos_ref, None, n_slots).astype(BF16)
    rows = jnp.dot(sel, h2_ref[...], preferred_element_type=F32)
    buf_ref[slot] = _pack_rows(rows)
    _segments(n_ref, j, lambda e, n: seg_copy(j, slot, e, n).start())

    @pl.when(j == n_j - 1)
    def _():
        @pl.when(j >= 1)
        def _():
            _segments(n_ref, j - 1, lambda e, n: seg_copy(j - 1, 1 - slot, e, n).wait())
        _segments(n_ref, j, lambda e, n: seg_copy(j, slot, e, n).wait())


def _dispatch(seg_n, seg_off, seg_dst, tail_lo, tail_n, lpos, h2, n_rows):
    tile = min(ROUTE_ROWS, h2.shape[0])
    n_t = h2.shape[0] // tile
    n_slots = TOP_K * tile + N_EXPERTS * SEG_ALIGN
    return pl.pallas_call(
        _dispatch_body,
        grid_spec=pltpu.PrefetchScalarGridSpec(
            num_scalar_prefetch=5,
            grid=(n_t,),
            in_specs=[pl.BlockSpec((TOP_K, tile), lambda j, *_: (0, j)),
                      pl.BlockSpec((tile, D_MODEL), lambda j, *_: (j, 0))],
            out_specs=pl.BlockSpec(memory_space=pl.ANY),
            scratch_shapes=[pltpu.VMEM((2, n_slots, D_HALF), U32),
                            pltpu.VMEM((EXPERT_ROWS, D_HALF), U32),
                            pltpu.SemaphoreType.DMA((2,)), pltpu.SemaphoreType.DMA],
        ),
        out_shape=jax.ShapeDtypeStruct((n_rows, D_HALF), U32),
        compiler_params=_cparams(1),
        name="dispatch",
    )(seg_n, seg_off, seg_dst, tail_lo, tail_n, lpos, h2)


def _combine_body(n_ref, off_ref, dst_ref, lpos_ref, gate_ref, x1_ref, fg_ref, ys_ref,
                  y_ref, buf_ref, sem):
    j = pl.program_id(0)
    n_j = pl.num_programs(0)
    slot = j % 2
    n_slots = buf_ref.shape[1]

    def seg_copy(tile, sl, e, n):
        src = ys_ref.at[pl.ds(pl.multiple_of(dst_ref[tile * N_EXPERTS + e], SEG_ALIGN), n), :]
        dst = buf_ref.at[sl, pl.ds(pl.multiple_of(off_ref[tile * N_EXPERTS + e], SEG_ALIGN), n), :]
        return pltpu.make_async_copy(src, dst, sem.at[sl])

    @pl.when(j == 0)
    def _():
        buf_ref[...] = jnp.zeros_like(buf_ref)
        _segments(n_ref, 0, lambda e, n: seg_copy(0, 0, e, n).start())

    @pl.when(j + 1 < n_j)
    def _():
        _segments(n_ref, j + 1, lambda e, n: seg_copy(j + 1, 1 - slot, e, n).start())

    _segments(n_ref, j, lambda e, n: seg_copy(j, slot, e, n).wait())

    w = _slot_matrix(lpos_ref, gate_ref, n_slots).T.astype(BF16)
    left, right = _unpack_rows(buf_ref[slot])
    moe = jnp.concatenate(
        [jnp.dot(w, half, preferred_element_type=F32) for half in (left, right)], axis=1)
    y_ref[...] = _rms(x1_ref[...] + moe, fg_ref[...])


def _combine(seg_n, seg_off, seg_dst, lpos, gates, x1, final_g, ys):
    T = x1.shape[0]
    tile = min(ROUTE_ROWS, T)
    n_t = T // tile
    n_slots = TOP_K * tile + N_EXPERTS * SEG_ALIGN
    return pl.pallas_call(
        _combine_body,
        grid_spec=pltpu.PrefetchScalarGridSpec(
            num_scalar_prefetch=3,
            grid=(n_t,),
            in_specs=[pl.BlockSpec((TOP_K, tile), lambda j, *_: (0, j)),
                      pl.BlockSpec((TOP_K, tile), lambda j, *_: (0, j)),
                      pl.BlockSpec((tile, D_MODEL), lambda j, *_: (j, 0)),
                      pl.BlockSpec((1, D_MODEL), lambda j, *_: (0, 0)),
                      pl.BlockSpec(memory_space=pl.ANY)],
            out_specs=pl.BlockSpec((tile, D_MODEL), lambda j, *_: (j, 0)),
            scratch_shapes=[pltpu.VMEM((2, n_slots, D_HALF), U32), pltpu.SemaphoreType.DMA((2,))],
        ),
        out_shape=jax.ShapeDtypeStruct((T, D_MODEL), F32),
        compiler_params=_cparams(1),
        name="combine",
    )(seg_n, seg_off, seg_dst, lpos, gates, x1, final_g, ys)


def _expert_body(be_ref, nact_ref, xs_ref, wgu_ref, bgu_ref, wdn_ref, bdn_ref, ys_ref):
    @pl.when(pl.program_id(0) < nact_ref[0])
    def _():
        left, right = _unpack_rows(xs_ref[...])
        hgu = (jnp.dot(left, wgu_ref[0:D_HALF, :], preferred_element_type=F32)
               + jnp.dot(right, wgu_ref[D_HALF:, :], preferred_element_type=F32)) + bgu_ref[...]
        glu = jnp.minimum(hgu[:, :D_FF], SWIGLU_LIMIT)
        lin = jnp.clip(hgu[:, D_FF:], -SWIGLU_LIMIT, SWIGLU_LIMIT)
        act = glu * jax.nn.sigmoid(SWIGLU_ALPHA * glu) * (lin + 1.0)
        y = jnp.dot(act.astype(BF16), wdn_ref[...], preferred_element_type=F32) + bdn_ref[...]
        ys_ref[...] = _pack_rows(y.astype(BF16).astype(F32))


def _experts(block_expert, n_active, xs, w_gu, b_gu, w_dn, b_dn):
    n_rows = xs.shape[0]
    n_blocks = n_rows // EXPERT_ROWS
    blk = lambda i, be, na: (jnp.minimum(i, na[0] - 1), 0)
    per_e = lambda i, be, na: (be[i], 0, 0)
    return pl.pallas_call(
        _expert_body,
        grid_spec=pltpu.PrefetchScalarGridSpec(
            num_scalar_prefetch=2,
            grid=(n_blocks,),
            in_specs=[pl.BlockSpec((EXPERT_ROWS, D_HALF), blk),
                      pl.BlockSpec((None, D_MODEL, 2 * D_FF), per_e),
                      pl.BlockSpec((None, 1, 2 * D_FF), per_e),
                      pl.BlockSpec((None, D_FF, D_MODEL), per_e),
                      pl.BlockSpec((None, 1, D_MODEL), per_e)],
            out_specs=pl.BlockSpec((EXPERT_ROWS, D_HALF), blk),
        ),
        out_shape=jax.ShapeDtypeStruct((n_rows, D_HALF), U32),
        compiler_params=_cparams(1),
        name="experts",
    )(block_expert, n_active, xs, w_gu, b_gu, w_dn, b_dn)


def _rope_tables(seq_len):
    t = np.arange(seq_len)
    inv = 1.0 / (ROPE_THETA ** (np.arange(0, AXIS_DIM, 2, dtype=np.float32) / AXIS_DIM))
    inv = inv.astype(np.float32)
    ang_r = (t // GRID_W).astype(np.float32)[:, None] * inv[None, :]
    ang_c = (t % GRID_W).astype(np.float32)[:, None] * inv[None, :]
    ang = np.concatenate([ang_r, ang_r, ang_c, ang_c], axis=-1).astype(np.float32)
    cos = np.cos(ang).astype(np.float32)
    sin = np.sin(ang).astype(np.float32)
    sign = np.where((np.arange(HEAD_DIM) % AXIS_DIM) < HALF_AXIS, -1.0, 1.0).astype(np.float32)
    reps = LANES // HEAD_DIM
    return jnp.asarray(np.tile(cos, (1, reps))), jnp.asarray(np.tile(sin * sign, (1, reps)))


def _prepare(norm1_g, w_in, q_norm_g, k_norm_g, conv_dw_w, conv_dw_b, conv_ln_g, conv_ln_b, w_pw2,
             attn_out_g, conv_out_g, w_out, norm2_g, router_w, router_b, w_gate_up, b_gate_up,
             w_down, b_down, final_g):
    row = lambda a: a.reshape(1, -1).astype(F32)
    qk_gain = jnp.concatenate([jnp.tile(q_norm_g[0] * (math.log2(math.e) / math.sqrt(HEAD_DIM)), N_Q_HEADS),
                               jnp.tile(k_norm_g[0], N_KV_HEADS)]).reshape(1, D_QK)
    head = np.arange(LANES) // HEAD_DIM
    hsum = jnp.asarray(np.tile((head[:, None] == head[None, :]).astype(np.float32), (2, 1)), dtype=BF16)
    rw = jnp.pad(router_w[0].astype(F32), ((0, 0), (0, LANES - N_EXPERTS)))
    rw_hi = rw.astype(BF16)
    rw_lo = (rw - rw_hi.astype(F32)).astype(BF16)
    rb = jnp.pad(router_b[0].astype(F32), (0, LANES - N_EXPERTS)).reshape(1, LANES)
    low = np.tril(np.ones((N_EXPERTS, N_EXPERTS), np.float32), k=-1)
    bgu = b_gate_up[0]
    return dict(
        norm1_g=row(norm1_g[0]), w_in=w_in[0].astype(BF16), qk_gain=qk_gain, hsum=hsum,
        dw_w=conv_dw_w[0].astype(F32), dw_b=row(conv_dw_b[0]), ln_g=row(conv_ln_g[0]),
        ln_b=row(conv_ln_b[0]), w_pw2=w_pw2[0].astype(BF16), conv_out_g=row(conv_out_g[0]),
        attn_out_g=row(attn_out_g[0]), w_out=w_out[0].astype(BF16), norm2_g=row(norm2_g[0]),
        rw_hi=rw_hi, rw_lo=rw_lo, rb=rb, low=jnp.asarray(low, dtype=BF16),
        w_gu=_wprep(w_gate_up[0]),
        b_gu=jnp.concatenate([bgu[:, 0::2].reshape(N_EXPERTS, -1, FF_CHUNK),
                              bgu[:, 1::2].reshape(N_EXPERTS, -1, FF_CHUNK)],
                             axis=-1).reshape(N_EXPERTS, 1, 2 * D_FF),
        w_dn=w_down[0].astype(BF16), b_dn=b_down[0].reshape(N_EXPERTS, 1, D_MODEL),
        final_g=row(final_g),
    )


def _trunk(x, p):
    B, S, _ = x.shape
    T = B * S
    cos, sin = _rope_tables(S)
    q, k, v, u = _pre(x, p["norm1_g"], p["w_in"], p["qk_gain"], cos, sin, p["hsum"])
    tile = min(ROUTE_ROWS, S)
    tri = jnp.asarray(np.triu(np.ones((tile, tile), np.float32), k=1), dtype=BF16)
    x1, h2, lpos, gates, seg, cnt = _mix(q, k, v, u, x, dict(p, tri=tri))

    n_t = T // tile
    seg = seg[:, :, 0:3].astype(I32)
    seg_n, seg_base, seg_off = seg[:, :, 0], seg[:, :, 1], seg[:, :, 2]
    counts = cnt[:, 0].astype(I32)
    padded = ((counts + EXPERT_ROWS - 1) // EXPERT_ROWS) * EXPERT_ROWS
    pend = jnp.cumsum(padded)
    pstart = pend - padded
    seg_dst = pstart[None, :] + seg_base
    n_blocks = (T * TOP_K + n_t * N_EXPERTS * SEG_ALIGN) // EXPERT_ROWS + N_EXPERTS
    n_active = pend[-1] // EXPERT_ROWS
    blk_start = jnp.minimum(jnp.arange(n_blocks, dtype=I32), n_active - 1) * EXPERT_ROWS
    block_expert = jnp.sum((blk_start[:, None] >= pend[None, :]).astype(I32), axis=1)
    block_expert = jnp.minimum(block_expert, N_EXPERTS - 1).astype(I32)
    flat = lambda a: a.reshape(-1).astype(I32)

    xs = _dispatch(flat(seg_n), flat(seg_off), flat(seg_dst), flat(pstart + counts),
                   flat(padded - counts), lpos, h2, n_blocks * EXPERT_ROWS)
    ys = _experts(block_expert, n_active.reshape(1).astype(I32), xs,
                  p["w_gu"], p["b_gu"], p["w_dn"], p["b_dn"])
    y = _combine(flat(seg_n), flat(seg_off), flat(seg_dst), lpos, gates,
                 x1.reshape(T, D_MODEL), p["final_g"], ys)
    return y.reshape(B, S, D_MODEL)


def kernel(x_prompt, x_sample, norm1_g, w_in, q_norm_g, k_norm_g, conv_dw_w, conv_dw_b, conv_ln_g,
           conv_ln_b, w_pw2, attn_out_g, conv_out_g, w_out, norm2_g, router_w, router_b, w_gate_up,
           b_gate_up, w_down, b_down, final_g):
    p = _prepare(norm1_g, w_in, q_norm_g, k_norm_g, conv_dw_w, conv_dw_b, conv_ln_g, conv_ln_b,
                 w_pw2, attn_out_g, conv_out_g, w_out, norm2_g, router_w, router_b, w_gate_up,
                 b_gate_up, w_down, b_down, final_g)
    return (_trunk(x_prompt, p), _trunk(x_sample, p))
```

```python
import functools
import math

import numpy as np
import jax
import jax.numpy as jnp
from jax import lax
from jax.experimental import pallas as pl
from jax.experimental.pallas import tpu as pltpu

F32 = jnp.float32
BF16 = jnp.bfloat16
I32 = jnp.int32
U32 = jnp.uint32

D_MODEL = 1024
GRID_W = 64
HEAD_DIM = 64
N_Q_HEADS = 8
N_KV_HEADS = 2
N_GROUPS_PER_KV = N_Q_HEADS // N_KV_HEADS
D_ATTN = N_Q_HEADS * HEAD_DIM
D_KV = N_KV_HEADS * HEAD_DIM
D_QK = D_ATTN + D_KV
AXIS_DIM = HEAD_DIM // 2
HALF_AXIS = AXIS_DIM // 2
ROPE_THETA = 10000.0
D_CONV = D_MODEL - D_ATTN
CONV_K = 31
CONV_PAD = (CONV_K - 1) // 2
D_IN_PROJ = D_ATTN + 2 * D_KV + 2 * D_CONV
N_EXPERTS = 32
TOP_K = 4
D_FF = D_MODEL
SWIGLU_ALPHA = 1.702
SWIGLU_LIMIT = 7.0
EPS = 1e-6

LANES = 128
SUBLANES = 8
HALO = 16
NEG_INF = float("-inf")
D_HALF = D_MODEL // 2
SEG_ALIGN = SUBLANES

PRE_ROWS = 512
MIX_ROWS = 512
CONV_ROWS = 256
ROUTE_ROWS = 256
EXPERT_ROWS = 512
WPREP_ROWS = 512
FF_CHUNK = D_FF

VMEM_LIMIT = 56 * 1024 * 1024


def _cparams(n_axes):
    return pltpu.CompilerParams(dimension_semantics=("arbitrary",) * n_axes,
                                vmem_limit_bytes=VMEM_LIMIT)


def _rms(x, g):
    return x * lax.rsqrt(jnp.mean(x * x, axis=-1, keepdims=True) + EPS) * g


def _split_bf16(x):
    hi = x.astype(BF16)
    lo = (x - hi.astype(F32)).astype(BF16)
    return hi, lo


def _pack_rows(x):
    bits = lax.bitcast_convert_type(x, U32)
    return (bits[:, :D_HALF] & jnp.uint32(0xFFFF0000)) | (bits[:, D_HALF:] >> 16)


def _unpack_rows(w):
    left = lax.bitcast_convert_type(w & jnp.uint32(0xFFFF0000), F32).astype(BF16)
    right = lax.bitcast_convert_type(w << 16, F32).astype(BF16)
    return left, right


def _wprep_body(w_ref, perm_ref, o_ref):
    perm = perm_ref[...]
    group = 2 * LANES
    for g in range(2 * D_FF // group):
        blk = w_ref[:, g * group:(g + 1) * group].astype(BF16)
        r = jnp.dot(blk, perm, preferred_element_type=F32)
        c, half = divmod(g * LANES, FF_CHUNK)
        o_ref[:, 2 * c * FF_CHUNK + half:2 * c * FF_CHUNK + half + LANES] = r[:, :LANES].astype(BF16)
        o_ref[:, (2 * c + 1) * FF_CHUNK + half:(2 * c + 1) * FF_CHUNK + half + LANES] = (
            r[:, LANES:].astype(BF16))


def _wprep(w_gu):
    group = 2 * LANES
    i = np.arange(group)[:, None]
    c = np.arange(group)[None, :]
    perm = np.where(c < LANES, i == 2 * c, i == 2 * (c - LANES) + 1).astype(np.float32)
    return pl.pallas_call(
        _wprep_body,
        grid=(N_EXPERTS, D_MODEL // WPREP_ROWS),
        in_specs=[pl.BlockSpec((None, WPREP_ROWS, 2 * D_FF), lambda e, r: (e, r, 0)),
                  pl.BlockSpec((group, group), lambda e, r: (0, 0))],
        out_specs=pl.BlockSpec((None, WPREP_ROWS, 2 * D_FF), lambda e, r: (e, r, 0)),
        out_shape=jax.ShapeDtypeStruct((N_EXPERTS, D_MODEL, 2 * D_FF), BF16),
        compiler_params=_cparams(2),
        name="wprep",
    )(w_gu, jnp.asarray(perm, dtype=BF16))


def _pre_body(x_ref, g1_ref, win_ref, qkg_ref, cos_ref, sin_ref, hsum_ref,
              q_ref, k_ref, v_ref, u_ref):
    h = _rms(x_ref[...], g1_ref[...]).astype(BF16)
    z = jnp.dot(h, win_ref[...], preferred_element_type=F32)
    cos = cos_ref[...]
    sin = sin_ref[...]
    lane = lax.broadcasted_iota(I32, (1, LANES), 1)
    first_half = (lane % AXIS_DIM) < HALF_AXIS
    hsum = hsum_ref[...]
    for c in range(D_QK // LANES):
        zc = z[:, c * LANES:(c + 1) * LANES]
        hi, lo = _split_bf16(zc * zc)
        ss = jnp.dot(jnp.concatenate([hi, lo], axis=1), hsum, preferred_element_type=F32)
        zn = zc * lax.rsqrt(ss * (1.0 / HEAD_DIM) + EPS) * qkg_ref[:, c * LANES:(c + 1) * LANES]
        rot = jnp.where(first_half,
                        pltpu.roll(zn, LANES - HALF_AXIS, 1),
                        pltpu.roll(zn, HALF_AXIS, 1))
        out = (zn * cos + rot * sin).astype(BF16)
        if c < D_ATTN // LANES:
            q_ref[:, c * LANES:(c + 1) * LANES] = out
        else:
            k_ref[...] = out
    zv = z[:, D_QK:D_QK + D_KV]
    ones_col = jnp.where(lane == HEAD_DIM, 1.0, 0.0)
    for kv in range(N_KV_HEADS):
        zk = zv if kv == 0 else pltpu.roll(zv, LANES - kv * HEAD_DIM, 1)
        v_ref[:, kv * LANES:(kv + 1) * LANES] = jnp.where(lane < HEAD_DIM, zk, ones_col).astype(BF16)
    a = z[:, D_QK + D_KV:D_QK + D_KV + D_CONV]
    gate = z[:, D_QK + D_KV + D_CONV:]
    u_ref[...] = a * jax.nn.sigmoid(gate)


def _pre(x, g1, w_in, qk_gain, cos, sin, hsum):
    B, S, _ = x.shape
    rows = min(PRE_ROWS, S)
    grid = (B, S // rows)
    row_blk = lambda w: pl.BlockSpec((None, rows, w), lambda b, i: (b, i, 0))
    full = lambda a: pl.BlockSpec(a.shape, lambda b, i: (0,) * a.ndim)
    return pl.pallas_call(
        _pre_body,
        grid=grid,
        in_specs=[row_blk(D_MODEL), full(g1), full(w_in), full(qk_gain),
                  pl.BlockSpec((rows, LANES), lambda b, i: (i, 0)),
                  pl.BlockSpec((rows, LANES), lambda b, i: (i, 0)),
                  full(hsum)],
        out_specs=[row_blk(D_ATTN), row_blk(D_KV), row_blk(N_KV_HEADS * LANES), row_blk(D_CONV)],
        out_shape=[jax.ShapeDtypeStruct((B, S, D_ATTN), BF16),
                   jax.ShapeDtypeStruct((B, S, D_KV), BF16),
                   jax.ShapeDtypeStruct((B, S, N_KV_HEADS * LANES), BF16),
                   jax.ShapeDtypeStruct((B, S, D_CONV), F32)],
        compiler_params=_cparams(2),
        name="pre",
    )(x, g1, w_in, qk_gain, cos, sin, hsum)


def _attend(q_ref, k_ref, v_ref, attn_ref):
    nt = (((1,), (1,)), ((), ()))
    for kv in range(N_KV_HEADS):
        kh = k_ref[:, kv * HEAD_DIM:(kv + 1) * HEAD_DIM]
        vh = v_ref[:, kv * LANES:(kv + 1) * LANES]
        for g in range(N_GROUPS_PER_KV):
            h = kv * N_GROUPS_PER_KV + g
            qh = q_ref[:, h * HEAD_DIM:(h + 1) * HEAD_DIM]
            s = lax.dot_general(qh, kh, nt, preferred_element_type=F32)
            m = jnp.max(s, axis=-1, keepdims=True)
            p = jnp.exp2(s - m).astype(BF16)
            o = jnp.dot(p, vh, preferred_element_type=F32)
            attn_ref[:, h * HEAD_DIM:(h + 1) * HEAD_DIM] = (
                o[:, :HEAD_DIM] / o[:, HEAD_DIM:HEAD_DIM + 1]).astype(BF16)


def _top4(lg, iota):
    vals, sels = [], []
    for _ in range(TOP_K):
        m = jnp.max(lg, axis=0, keepdims=True)
        idx = jnp.min(jnp.where(lg == m, iota, N_EXPERTS), axis=0, keepdims=True)
        sel = iota == idx
        vals.append(m)
        sels.append(sel)
        lg = jnp.where(sel, NEG_INF, lg)
    return vals, sels


def _route(lg, tri_ref, low_ref, run_ref, lpos_ref, gate_ref, seg_ref, cols):
    tile = lg.shape[0]
    lgt = lg.T[0:N_EXPERTS, :]
    eio = lax.broadcasted_iota(I32, (N_EXPERTS, tile), 0)
    vals, sels = _top4(lgt, eio)
    es = [jnp.exp(v - vals[0]) for v in vals]
    inv = 1.0 / (es[0] + es[1] + es[2] + es[3])
    onehot = jnp.where(sels[0] | sels[1] | sels[2] | sels[3], 1.0, 0.0)
    prefix = jnp.dot(onehot.astype(BF16), tri_ref[...], preferred_element_type=F32)
    n_tok = jnp.sum(onehot, axis=1, keepdims=True)
    n_seg = jnp.maximum(jnp.floor((n_tok + (SEG_ALIGN - 1)) * (1.0 / SEG_ALIGN)), 1.0) * SEG_ALIGN
    off = jnp.dot(low_ref[...], jnp.broadcast_to(n_seg, (N_EXPERTS, LANES)).astype(BF16),
                  preferred_element_type=F32)
    pos = off[:, 0:1] + prefix
    for kk in range(TOP_K):
        gate_ref[kk:kk + 1, cols] = es[kk] * inv
        lpos_ref[kk:kk + 1, cols] = jnp.sum(jnp.where(sels[kk], pos, 0.0), axis=0,
                                            keepdims=True).astype(I32)
    lane = lax.broadcasted_iota(I32, (N_EXPERTS, LANES), 1)
    seg_ref[...] = jnp.where(lane == 0, n_seg, jnp.where(lane == 1, run_ref[...],
                                                         jnp.where(lane == 2, off, 0.0)))
    run_ref[...] = run_ref[...] + n_seg


def _mix_body(q_ref, k_ref, v_ref, up_ref, um_ref, un_ref, x_ref,
              dww_ref, dwb_ref, lng_ref, lnb_ref, wpw_ref, cog_ref, aog_ref, wout_ref,
              g2_ref, rwh_ref, rwl_ref, rb_ref, tri_ref, low_ref,
              x1_ref, h2_ref, lpos_ref, gate_ref, seg_ref, cnt_ref,
              attn_ref, wpad_ref, shift_ref, conv_ref, run_ref):
    b = pl.program_id(0)
    i = pl.program_id(1)
    n_i = pl.num_programs(1)
    rows = um_ref.shape[0]
    tile = tri_ref.shape[0]

    @pl.when((b == 0) & (i == 0))
    def _():
        run_ref[...] = jnp.zeros_like(run_ref)

    wpad_ref[0:HALO, :] = jnp.where(i > 0, up_ref[...], 0.0)
    wpad_ref[HALO:HALO + rows, :] = um_ref[...]
    wpad_ref[HALO + rows:, :] = jnp.where(i < n_i - 1, un_ref[...], 0.0)
    span = rows + 2 * HALO - SUBLANES
    for r in range(1, SUBLANES):
        shift_ref[r - 1] = wpad_ref[r:r + span, :]
    part = min(CONV_ROWS, rows)
    for c in range(D_CONV // LANES):
        cs = slice(c * LANES, (c + 1) * LANES)
        for r0 in range(0, rows, part):
            acc = jnp.zeros((part, LANES), F32)
            for j in range(CONV_K):
                off = HALO - CONV_PAD + j
                r, base = off % SUBLANES, r0 + off - off % SUBLANES
                tap = (wpad_ref[base:base + part, cs] if r == 0
                       else shift_ref[r - 1, base:base + part, cs])
                acc = acc + tap * dww_ref[j:j + 1, cs]
            conv_ref[r0:r0 + part, cs] = acc + dwb_ref[:, cs]
    _attend(q_ref, k_ref, v_ref, attn_ref)
    y = conv_ref[...]
    mu = jnp.mean(y, axis=-1, keepdims=True)
    yc = y - mu
    var = jnp.mean(yc * yc, axis=-1, keepdims=True)
    y = yc * lax.rsqrt(var + EPS) * lng_ref[...] + lnb_ref[...]
    y = y * jax.nn.sigmoid(y)
    conv_out = jnp.dot(y.astype(BF16), wpw_ref[...], preferred_element_type=F32)

    cn = _rms(conv_out, cog_ref[...]).astype(BF16)
    an = _rms(attn_ref[...].astype(F32), aog_ref[...]).astype(BF16)
    mix = (jnp.dot(cn, wout_ref[0:D_CONV, :], preferred_element_type=F32)
           + jnp.dot(an, wout_ref[D_CONV:, :], preferred_element_type=F32))
    x1 = x_ref[...] + mix
    x1_ref[...] = x1
    h2 = _rms(x1, g2_ref[...])
    hi, lo = _split_bf16(h2)
    h2_ref[...] = hi

    rwh = rwh_ref[...]
    lg = (jnp.dot(hi, rwh, preferred_element_type=F32)
          + jnp.dot(lo, rwh, preferred_element_type=F32)
          + jnp.dot(hi, rwl_ref[...], preferred_element_type=F32)) + rb_ref[...]
    for t in range(rows // tile):
        cols = slice(t * tile, (t + 1) * tile)
        _route(lg[cols, :], tri_ref, low_ref, run_ref, lpos_ref, gate_ref, seg_ref.at[t], cols)
    cnt_ref[...] = run_ref[...]


def _mix(q, k, v, u, x, p):
    B, S, _ = x.shape
    rows = min(MIX_ROWS, S)
    tile = p["tri"].shape[0]
    n_i = S // rows
    T = B * S
    hb = rows // HALO
    n_halo = S // HALO
    full = lambda a: pl.BlockSpec(a.shape, lambda b, i: (0,) * a.ndim)
    row_blk = lambda w: pl.BlockSpec((None, rows, w), lambda b, i: (b, i, 0))
    seq_blk = lambda w: pl.BlockSpec((None, S, w), lambda b, i: (b, 0, 0))
    tok_blk = lambda r: pl.BlockSpec((r, rows), lambda b, i: (0, b * n_i + i))
    weights = [p["dw_w"], p["dw_b"], p["ln_g"], p["ln_b"], p["w_pw2"], p["conv_out_g"],
               p["attn_out_g"], p["w_out"], p["norm2_g"], p["rw_hi"], p["rw_lo"], p["rb"],
               p["tri"], p["low"]]
    return pl.pallas_call(
        _mix_body,
        grid=(B, n_i),
        in_specs=[row_blk(D_ATTN), seq_blk(D_KV), seq_blk(N_KV_HEADS * LANES),
                  pl.BlockSpec((None, HALO, D_CONV), lambda b, i: (b, jnp.maximum(i * hb - 1, 0), 0)),
                  row_blk(D_CONV),
                  pl.BlockSpec((None, HALO, D_CONV),
                               lambda b, i: (b, jnp.minimum((i + 1) * hb, n_halo - 1), 0)),
                  row_blk(D_MODEL)] + [full(w) for w in weights],
        out_specs=[row_blk(D_MODEL),
                   pl.BlockSpec((rows, D_MODEL), lambda b, i: (b * n_i + i, 0)),
                   tok_blk(TOP_K), tok_blk(TOP_K),
                   pl.BlockSpec((rows // tile, N_EXPERTS, LANES), lambda b, i: (b * n_i + i, 0, 0)),
                   pl.BlockSpec((N_EXPERTS, LANES), lambda b, i: (0, 0))],
        out_shape=[jax.ShapeDtypeStruct((B, S, D_MODEL), F32),
                   jax.ShapeDtypeStruct((T, D_MODEL), BF16),
                   jax.ShapeDtypeStruct((TOP_K, T), I32),
                   jax.ShapeDtypeStruct((TOP_K, T), F32),
                   jax.ShapeDtypeStruct((T // tile, N_EXPERTS, LANES), F32),
                   jax.ShapeDtypeStruct((N_EXPERTS, LANES), F32)],
        scratch_shapes=[pltpu.VMEM((rows, D_ATTN), BF16),
                        pltpu.VMEM((rows + 2 * HALO, D_CONV), F32),
                        pltpu.VMEM((SUBLANES - 1, rows + 2 * HALO - SUBLANES, D_CONV), F32),
                        pltpu.VMEM((rows, D_CONV), F32),
                        pltpu.VMEM((N_EXPERTS, LANES), F32)],
        compiler_params=_cparams(2),
        name="mix",
    )(q, k, v, u, u, u, x, *weights)


def _segments(n_ref, tile, fn, maybe_empty=False):
    for e in range(N_EXPERTS):
        n = n_ref[tile * N_EXPERTS + e]
        if maybe_empty:
            pl.when(n > 0)(functools.partial(fn, e, pl.multiple_of(n, SEG_ALIGN)))
        else:
            fn(e, pl.multiple_of(n, SEG_ALIGN))


def _tile_rows(n_ref, off_ref, tile):
    last = tile * N_EXPERTS + N_EXPERTS - 1
    return pl.multiple_of(off_ref[last] + n_ref[last], SEG_ALIGN)


def _slot_matrix(lpos_ref, vals, n_slots):
    tile = lpos_ref.shape[1]
    sio = lax.broadcasted_iota(I32, (n_slots, tile), 0)
    m = jnp.zeros((n_slots, tile), F32)
    for kk in range(TOP_K):
        v = 1.0 if vals is None else vals[kk:kk + 1, :]
        m = jnp.where(sio == lpos_ref[kk:kk + 1, :], v, m)
    return m


def _dispatch_body(n_ref, off_ref, dst_ref, tlo_ref, tn_ref,
                   lpos_ref, h2_ref, xs_ref, buf_ref, zero_ref, sem, zsem):
    j = pl.program_id(0)
    n_j = pl.num_programs(0)
    slot = j % 2
    n_slots = buf_ref.shape[1]

    def seg_copy(tile, sl, e, n):
        src = buf_ref.at[sl, pl.ds(pl.multiple_of(off_ref[tile * N_EXPERTS + e], SEG_ALIGN), n), :]
        dst = xs_ref.at[pl.ds(pl.multiple_of(dst_ref[tile * N_EXPERTS + e], SEG_ALIGN), n), :]
        return pltpu.make_async_copy(src, dst, sem.at[sl])

    def tail_copy(e, n):
        return pltpu.make_async_copy(
            zero_ref.at[pl.ds(0, n), :],
            xs_ref.at[pl.ds(pl.multiple_of(tlo_ref[e], SEG_ALIGN), n), :], zsem)

    @pl.when(j == 0)
    def _():
        zero_ref[...] = jnp.zeros_like(zero_ref)
        _segments(tn_ref, 0, lambda e, n: tail_copy(e, n).start(), maybe_empty=True)
        _segments(tn_ref, 0, lambda e, n: tail_copy(e, n).wait(), maybe_empty=True)

    def wait_tile(tile, sl):
        total = _tile_rows(n_ref, off_ref, tile)
        pltpu.make_async_copy(buf_ref.at[sl, pl.ds(0, total), :], xs_ref.at[pl.ds(0, total), :],
                              sem.at[sl]).wait()

    @pl.when(j >= 2)
    def _():
        wait_tile(j - 2, slot)

    sel = _slot_matrix(lpos_ref, None, n_slots).astype(BF16)
    rows = jnp.dot(sel, h2_ref[...], preferred_element_type=F32)
    buf_ref[slot] = _pack_rows(rows)
    _segments(n_ref, j, lambda e, n: seg_copy(j, slot, e, n).start())

    @pl.when(j == n_j - 1)
    def _():
        @pl.when(j >= 1)
        def _():
            wait_tile(j - 1, 1 - slot)
        wait_tile(j, slot)


def _dispatch(seg_n, seg_off, seg_dst, tail_lo, tail_n, lpos, h2, n_rows):
    tile = min(ROUTE_ROWS, h2.shape[0])
    n_t = h2.shape[0] // tile
    n_slots = TOP_K * tile + N_EXPERTS * SEG_ALIGN
    return pl.pallas_call(
        _dispatch_body,
        grid_spec=pltpu.PrefetchScalarGridSpec(
            num_scalar_prefetch=5,
            grid=(n_t,),
            in_specs=[pl.BlockSpec((TOP_K, tile), lambda j, *_: (0, j)),
                      pl.BlockSpec((tile, D_MODEL), lambda j, *_: (j, 0))],
            out_specs=pl.BlockSpec(memory_space=pl.ANY),
            scratch_shapes=[pltpu.VMEM((2, n_slots, D_HALF), U32),
                            pltpu.VMEM((EXPERT_ROWS, D_HALF), U32),
                            pltpu.SemaphoreType.DMA((2,)), pltpu.SemaphoreType.DMA],
        ),
        out_shape=jax.ShapeDtypeStruct((n_rows, D_HALF), U32),
        compiler_params=_cparams(1),
        name="dispatch",
    )(seg_n, seg_off, seg_dst, tail_lo, tail_n, lpos, h2)


def _combine_body(n_ref, off_ref, dst_ref, lpos_ref, gate_ref, x1_ref, fg_ref, ys_ref,
                  y_ref, buf_ref, sem):
    j = pl.program_id(0)
    n_j = pl.num_programs(0)
    slot = j % 2
    n_slots = buf_ref.shape[1]

    def seg_copy(tile, sl, e, n):
        src = ys_ref.at[pl.ds(pl.multiple_of(dst_ref[tile * N_EXPERTS + e], SEG_ALIGN), n), :]
        dst = buf_ref.at[sl, pl.ds(pl.multiple_of(off_ref[tile * N_EXPERTS + e], SEG_ALIGN), n), :]
        return pltpu.make_async_copy(src, dst, sem.at[sl])

    @pl.when(j == 0)
    def _():
        buf_ref[...] = jnp.zeros_like(buf_ref)
        _segments(n_ref, 0, lambda e, n: seg_copy(0, 0, e, n).start())

    def wait_tile(tile, sl):
        total = _tile_rows(n_ref, off_ref, tile)
        pltpu.make_async_copy(ys_ref.at[pl.ds(0, total), :], buf_ref.at[sl, pl.ds(0, total), :],
                              sem.at[sl]).wait()

    nxt = jnp.minimum(j + 1, n_j - 1)
    _segments(n_ref, nxt, lambda e, n: seg_copy(nxt, 1 - slot, e, n).start())
    w = _slot_matrix(lpos_ref, gate_ref, n_slots).T.astype(BF16)
    wait_tile(j, slot)
    left, right = _unpack_rows(buf_ref[slot])
    moe = jnp.concatenate(
        [jnp.dot(w, half, preferred_element_type=F32) for half in (left, right)], axis=1)
    y_ref[...] = _rms(x1_ref[...] + moe, fg_ref[...])

    @pl.when(j == n_j - 1)
    def _():
        wait_tile(nxt, 1 - slot)


def _combine(seg_n, seg_off, seg_dst, lpos, gates, x1, final_g, ys):
    T = x1.shape[0]
    tile = min(ROUTE_ROWS, T)
    n_t = T // tile
    n_slots = TOP_K * tile + N_EXPERTS * SEG_ALIGN
    return pl.pallas_call(
        _combine_body,
        grid_spec=pltpu.PrefetchScalarGridSpec(
            num_scalar_prefetch=3,
            grid=(n_t,),
            in_specs=[pl.BlockSpec((TOP_K, tile), lambda j, *_: (0, j)),
                      pl.BlockSpec((TOP_K, tile), lambda j, *_: (0, j)),
                      pl.BlockSpec((tile, D_MODEL), lambda j, *_: (j, 0)),
                      pl.BlockSpec((1, D_MODEL), lambda j, *_: (0, 0)),
                      pl.BlockSpec(memory_space=pl.ANY)],
            out_specs=pl.BlockSpec((tile, D_MODEL), lambda j, *_: (j, 0)),
            scratch_shapes=[pltpu.VMEM((2, n_slots, D_HALF), U32), pltpu.SemaphoreType.DMA((2,))],
        ),
        out_shape=jax.ShapeDtypeStruct((T, D_MODEL), F32),
        compiler_params=_cparams(1),
        name="combine",
    )(seg_n, seg_off, seg_dst, lpos, gates, x1, final_g, ys)


def _expert_body(be_ref, nact_ref, xs_ref, wgu_ref, bgu_ref, wdn_ref, bdn_ref, ys_ref):
    @pl.when(pl.program_id(0) < nact_ref[0])
    def _():
        left, right = _unpack_rows(xs_ref[...])
        hgu = (jnp.dot(left, wgu_ref[0:D_HALF, :], preferred_element_type=F32)
               + jnp.dot(right, wgu_ref[D_HALF:, :], preferred_element_type=F32)) + bgu_ref[...]
        glu = jnp.minimum(hgu[:, :D_FF], SWIGLU_LIMIT)
        lin = jnp.clip(hgu[:, D_FF:], -SWIGLU_LIMIT, SWIGLU_LIMIT)
        act = glu * jax.nn.sigmoid(SWIGLU_ALPHA * glu) * (lin + 1.0)
        y = jnp.dot(act.astype(BF16), wdn_ref[...], preferred_element_type=F32) + bdn_ref[...]
        ys_ref[...] = _pack_rows(y.astype(BF16).astype(F32))


def _experts(block_expert, n_active, xs, w_gu, b_gu, w_dn, b_dn):
    n_rows = xs.shape[0]
    n_blocks = n_rows // EXPERT_ROWS
    blk = lambda i, be, na: (jnp.minimum(i, na[0] - 1), 0)
    per_e = lambda i, be, na: (be[i], 0, 0)
    return pl.pallas_call(
        _expert_body,
        grid_spec=pltpu.PrefetchScalarGridSpec(
            num_scalar_prefetch=2,
            grid=(n_blocks,),
            in_specs=[pl.BlockSpec((EXPERT_ROWS, D_HALF), blk),
                      pl.BlockSpec((None, D_MODEL, 2 * D_FF), per_e),
                      pl.BlockSpec((None, 1, 2 * D_FF), per_e),
                      pl.BlockSpec((None, D_FF, D_MODEL), per_e),
                      pl.BlockSpec((None, 1, D_MODEL), per_e)],
            out_specs=pl.BlockSpec((EXPERT_ROWS, D_HALF), blk),
        ),
        out_shape=jax.ShapeDtypeStruct((n_rows, D_HALF), U32),
        compiler_params=_cparams(1),
        name="experts",
    )(block_expert, n_active, xs, w_gu, b_gu, w_dn, b_dn)


def _rope_tables(seq_len):
    t = np.arange(seq_len)
    inv = 1.0 / (ROPE_THETA ** (np.arange(0, AXIS_DIM, 2, dtype=np.float32) / AXIS_DIM))
    inv = inv.astype(np.float32)
    ang_r = (t // GRID_W).astype(np.float32)[:, None] * inv[None, :]
    ang_c = (t % GRID_W).astype(np.float32)[:, None] * inv[None, :]
    ang = np.concatenate([ang_r, ang_r, ang_c, ang_c], axis=-1).astype(np.float32)
    cos = np.cos(ang).astype(np.float32)
    sin = np.sin(ang).astype(np.float32)
    sign = np.where((np.arange(HEAD_DIM) % AXIS_DIM) < HALF_AXIS, -1.0, 1.0).astype(np.float32)
    reps = LANES // HEAD_DIM
    return jnp.asarray(np.tile(cos, (1, reps))), jnp.asarray(np.tile(sin * sign, (1, reps)))


def _prepare(norm1_g, w_in, q_norm_g, k_norm_g, conv_dw_w, conv_dw_b, conv_ln_g, conv_ln_b, w_pw2,
             attn_out_g, conv_out_g, w_out, norm2_g, router_w, router_b, w_gate_up, b_gate_up,
             w_down, b_down, final_g):
    row = lambda a: a.reshape(1, -1).astype(F32)
    qk_gain = jnp.concatenate([jnp.tile(q_norm_g[0] * (math.log2(math.e) / math.sqrt(HEAD_DIM)), N_Q_HEADS),
                               jnp.tile(k_norm_g[0], N_KV_HEADS)]).reshape(1, D_QK)
    head = np.arange(LANES) // HEAD_DIM
    hsum = jnp.asarray(np.tile((head[:, None] == head[None, :]).astype(np.float32), (2, 1)), dtype=BF16)
    rw = jnp.pad(router_w[0].astype(F32), ((0, 0), (0, LANES - N_EXPERTS)))
    rw_hi = rw.astype(BF16)
    rw_lo = (rw - rw_hi.astype(F32)).astype(BF16)
    rb = jnp.pad(router_b[0].astype(F32), (0, LANES - N_EXPERTS)).reshape(1, LANES)
    low = np.tril(np.ones((N_EXPERTS, N_EXPERTS), np.float32), k=-1)
    bgu = b_gate_up[0]
    return dict(
        norm1_g=row(norm1_g[0]), w_in=w_in[0].astype(BF16), qk_gain=qk_gain, hsum=hsum,
        dw_w=conv_dw_w[0].astype(F32), dw_b=row(conv_dw_b[0]), ln_g=row(conv_ln_g[0]),
        ln_b=row(conv_ln_b[0]), w_pw2=w_pw2[0].astype(BF16), conv_out_g=row(conv_out_g[0]),
        attn_out_g=row(attn_out_g[0]), w_out=w_out[0].astype(BF16), norm2_g=row(norm2_g[0]),
        rw_hi=rw_hi, rw_lo=rw_lo, rb=rb, low=jnp.asarray(low, dtype=BF16),
        w_gu=_wprep(w_gate_up[0]),
        b_gu=jnp.concatenate([bgu[:, 0::2].reshape(N_EXPERTS, -1, FF_CHUNK),
                              bgu[:, 1::2].reshape(N_EXPERTS, -1, FF_CHUNK)],
                             axis=-1).reshape(N_EXPERTS, 1, 2 * D_FF),
        w_dn=w_down[0].astype(BF16), b_dn=b_down[0].reshape(N_EXPERTS, 1, D_MODEL),
        final_g=row(final_g),
    )


def _trunk(x, p):
    B, S, _ = x.shape
    T = B * S
    cos, sin = _rope_tables(S)
    q, k, v, u = _pre(x, p["norm1_g"], p["w_in"], p["qk_gain"], cos, sin, p["hsum"])
    tile = min(ROUTE_ROWS, S)
    tri = jnp.asarray(np.triu(np.ones((tile, tile), np.float32), k=1), dtype=BF16)
    x1, h2, lpos, gates, seg, cnt = _mix(q, k, v, u, x, dict(p, tri=tri))

    n_t = T // tile
    seg = seg[:, :, 0:3].astype(I32)
    seg_n, seg_base, seg_off = seg[:, :, 0], seg[:, :, 1], seg[:, :, 2]
    counts = cnt[:, 0].astype(I32)
    padded = ((counts + EXPERT_ROWS - 1) // EXPERT_ROWS) * EXPERT_ROWS
    pend = jnp.cumsum(padded)
    pstart = pend - padded
    seg_dst = pstart[None, :] + seg_base
    n_blocks = (T * TOP_K + n_t * N_EXPERTS * SEG_ALIGN) // EXPERT_ROWS + N_EXPERTS
    n_active = pend[-1] // EXPERT_ROWS
    blk_start = jnp.minimum(jnp.arange(n_blocks, dtype=I32), n_active - 1) * EXPERT_ROWS
    block_expert = jnp.sum((blk_start[:, None] >= pend[None, :]).astype(I32), axis=1)
    block_expert = jnp.minimum(block_expert, N_EXPERTS - 1).astype(I32)
    flat = lambda a: a.reshape(-1).astype(I32)

    xs = _dispatch(flat(seg_n), flat(seg_off), flat(seg_dst), flat(pstart + counts),
                   flat(padded - counts), lpos, h2, n_blocks * EXPERT_ROWS)
    ys = _experts(block_expert, n_active.reshape(1).astype(I32), xs,
                  p["w_gu"], p["b_gu"], p["w_dn"], p["b_dn"])
    y = _combine(flat(seg_n), flat(seg_off), flat(seg_dst), lpos, gates,
                 x1.reshape(T, D_MODEL), p["final_g"], ys)
    return y.reshape(B, S, D_MODEL)


def kernel(x_prompt, x_sample, norm1_g, w_in, q_norm_g, k_norm_g, conv_dw_w, conv_dw_b, conv_ln_g,
           conv_ln_b, w_pw2, attn_out_g, conv_out_g, w_out, norm2_g, router_w, router_b, w_gate_up,
           b_gate_up, w_down, b_down, final_g):
    p = _prepare(norm1_g, w_in, q_norm_g, k_norm_g, conv_dw_w, conv_dw_b, conv_ln_g, conv_ln_b,
                 w_pw2, attn_out_g, conv_out_g, w_out, norm2_g, router_w, router_b, w_gate_up,
                 b_gate_up, w_down, b_down, final_g)
    return (_trunk(x_prompt, p), _trunk(x_sample, p))
```

```python
import functools
import math

import numpy as np
import jax
import jax.numpy as jnp
from jax import lax
from jax.experimental import pallas as pl
from jax.experimental.pallas import tpu as pltpu

F32 = jnp.float32
BF16 = jnp.bfloat16
I32 = jnp.int32
U32 = jnp.uint32

D_MODEL = 1024
GRID_W = 64
HEAD_DIM = 64
N_Q_HEADS = 8
N_KV_HEADS = 2
N_GROUPS_PER_KV = N_Q_HEADS // N_KV_HEADS
D_ATTN = N_Q_HEADS * HEAD_DIM
D_KV = N_KV_HEADS * HEAD_DIM
D_QK = D_ATTN + D_KV
AXIS_DIM = HEAD_DIM // 2
HALF_AXIS = AXIS_DIM // 2
ROPE_THETA = 10000.0
D_CONV = D_MODEL - D_ATTN
CONV_K = 31
CONV_PAD = (CONV_K - 1) // 2
D_IN_PROJ = D_ATTN + 2 * D_KV + 2 * D_CONV
N_EXPERTS = 32
TOP_K = 4
D_FF = D_MODEL
SWIGLU_ALPHA = 1.702
SWIGLU_LIMIT = 7.0
EPS = 1e-6

LANES = 128
SUBLANES = 8
HALO = 16
NEG_INF = float("-inf")
D_HALF = D_MODEL // 2
SEG_ALIGN = SUBLANES

PRE_ROWS = 1024
MIX_ROWS = 512
CONV_ROWS = 256
ROUTE_ROWS = 256
EXPERT_ROWS = 512
WPREP_ROWS = 512
FF_CHUNK = D_FF

VMEM_LIMIT = 56 * 1024 * 1024


def _cparams(n_axes):
    return pltpu.CompilerParams(dimension_semantics=("arbitrary",) * n_axes,
                                vmem_limit_bytes=VMEM_LIMIT)


def _rms(x, g):
    return x * lax.rsqrt(jnp.mean(x * x, axis=-1, keepdims=True) + EPS) * g


def _split_bf16(x):
    hi = x.astype(BF16)
    lo = (x - hi.astype(F32)).astype(BF16)
    return hi, lo


def _pack_rows(x):
    bits = lax.bitcast_convert_type(x, U32)
    return (bits[:, :D_HALF] & jnp.uint32(0xFFFF0000)) | (bits[:, D_HALF:] >> 16)


def _unpack_rows(w):
    left = lax.bitcast_convert_type(w & jnp.uint32(0xFFFF0000), F32).astype(BF16)
    right = lax.bitcast_convert_type(w << 16, F32).astype(BF16)
    return left, right


def _wprep_body(w_ref, perm_ref, o_ref):
    perm = perm_ref[...]
    group = 2 * LANES
    for g in range(2 * D_FF // group):
        blk = w_ref[:, g * group:(g + 1) * group].astype(BF16)
        r = jnp.dot(blk, perm, preferred_element_type=F32)
        c, half = divmod(g * LANES, FF_CHUNK)
        o_ref[:, 2 * c * FF_CHUNK + half:2 * c * FF_CHUNK + half + LANES] = r[:, :LANES].astype(BF16)
        o_ref[:, (2 * c + 1) * FF_CHUNK + half:(2 * c + 1) * FF_CHUNK + half + LANES] = (
            r[:, LANES:].astype(BF16))


def _wprep(w_gu):
    group = 2 * LANES
    i = np.arange(group)[:, None]
    c = np.arange(group)[None, :]
    perm = np.where(c < LANES, i == 2 * c, i == 2 * (c - LANES) + 1).astype(np.float32)
    return pl.pallas_call(
        _wprep_body,
        grid=(N_EXPERTS, D_MODEL // WPREP_ROWS),
        in_specs=[pl.BlockSpec((None, WPREP_ROWS, 2 * D_FF), lambda e, r: (e, r, 0)),
                  pl.BlockSpec((group, group), lambda e, r: (0, 0))],
        out_specs=pl.BlockSpec((None, WPREP_ROWS, 2 * D_FF), lambda e, r: (e, r, 0)),
        out_shape=jax.ShapeDtypeStruct((N_EXPERTS, D_MODEL, 2 * D_FF), BF16),
        compiler_params=_cparams(2),
        name="wprep",
    )(w_gu, jnp.asarray(perm, dtype=BF16))


def _pre_body(x_ref, g1_ref, win_ref, qkg_ref, cos_ref, sin_ref, hsum_ref,
              q_ref, k_ref, v_ref, u_ref):
    h = _rms(x_ref[...], g1_ref[...]).astype(BF16)
    z = jnp.dot(h, win_ref[...], preferred_element_type=F32)
    cos = cos_ref[...]
    sin = sin_ref[...]
    lane = lax.broadcasted_iota(I32, (1, LANES), 1)
    first_half = (lane % AXIS_DIM) < HALF_AXIS
    hsum = hsum_ref[...]
    for c in range(D_QK // LANES):
        zc = z[:, c * LANES:(c + 1) * LANES]
        hi, lo = _split_bf16(zc * zc)
        ss = jnp.dot(jnp.concatenate([hi, lo], axis=1), hsum, preferred_element_type=F32)
        zn = zc * lax.rsqrt(ss * (1.0 / HEAD_DIM) + EPS) * qkg_ref[:, c * LANES:(c + 1) * LANES]
        rot = jnp.where(first_half,
                        pltpu.roll(zn, LANES - HALF_AXIS, 1),
                        pltpu.roll(zn, HALF_AXIS, 1))
        out = (zn * cos + rot * sin).astype(BF16)
        if c < D_ATTN // LANES:
            q_ref[:, c * LANES:(c + 1) * LANES] = out
        else:
            k_ref[...] = out
    zv = z[:, D_QK:D_QK + D_KV]
    ones_col = jnp.where(lane == HEAD_DIM, 1.0, 0.0)
    for kv in range(N_KV_HEADS):
        zk = zv if kv == 0 else pltpu.roll(zv, LANES - kv * HEAD_DIM, 1)
        v_ref[:, kv * LANES:(kv + 1) * LANES] = jnp.where(lane < HEAD_DIM, zk, ones_col).astype(BF16)
    a = z[:, D_QK + D_KV:D_QK + D_KV + D_CONV]
    gate = z[:, D_QK + D_KV + D_CONV:]
    u_ref[...] = a * jax.nn.sigmoid(gate)


def _pre(x, g1, w_in, qk_gain, cos, sin, hsum):
    B, S, _ = x.shape
    rows = min(PRE_ROWS, S)
    grid = (B, S // rows)
    row_blk = lambda w: pl.BlockSpec((None, rows, w), lambda b, i: (b, i, 0))
    full = lambda a: pl.BlockSpec(a.shape, lambda b, i: (0,) * a.ndim)
    return pl.pallas_call(
        _pre_body,
        grid=grid,
        in_specs=[row_blk(D_MODEL), full(g1), full(w_in), full(qk_gain),
                  pl.BlockSpec((rows, LANES), lambda b, i: (i, 0)),
                  pl.BlockSpec((rows, LANES), lambda b, i: (i, 0)),
                  full(hsum)],
        out_specs=[row_blk(D_ATTN), row_blk(D_KV), row_blk(N_KV_HEADS * LANES), row_blk(D_CONV)],
        out_shape=[jax.ShapeDtypeStruct((B, S, D_ATTN), BF16),
                   jax.ShapeDtypeStruct((B, S, D_KV), BF16),
                   jax.ShapeDtypeStruct((B, S, N_KV_HEADS * LANES), BF16),
                   jax.ShapeDtypeStruct((B, S, D_CONV), F32)],
        compiler_params=_cparams(2),
        name="pre",
    )(x, g1, w_in, qk_gain, cos, sin, hsum)


def _attend(q_ref, k_ref, v_ref, attn_ref):
    nt = (((1,), (1,)), ((), ()))
    for kv in range(N_KV_HEADS):
        kh = k_ref[:, kv * HEAD_DIM:(kv + 1) * HEAD_DIM]
        vh = v_ref[:, kv * LANES:(kv + 1) * LANES]
        for g in range(N_GROUPS_PER_KV):
            h = kv * N_GROUPS_PER_KV + g
            qh = q_ref[:, h * HEAD_DIM:(h + 1) * HEAD_DIM]
            s = lax.dot_general(qh, kh, nt, preferred_element_type=F32)
            m = jnp.max(s, axis=-1, keepdims=True)
            p = jnp.exp2(s - m).astype(BF16)
            o = jnp.dot(p, vh, preferred_element_type=F32)
            attn_ref[:, h * HEAD_DIM:(h + 1) * HEAD_DIM] = (
                o[:, :HEAD_DIM] / o[:, HEAD_DIM:HEAD_DIM + 1]).astype(BF16)


def _top4(lg, iota):
    vals, sels = [], []
    for _ in range(TOP_K):
        m = jnp.max(lg, axis=0, keepdims=True)
        idx = jnp.min(jnp.where(lg == m, iota, N_EXPERTS), axis=0, keepdims=True)
        sel = iota == idx
        vals.append(m)
        sels.append(sel)
        lg = jnp.where(sel, NEG_INF, lg)
    return vals, sels


def _route(lg, tri_ref, low_ref, run_ref, lpos_ref, gate_ref, seg_ref, cols):
    tile = lg.shape[0]
    lgt = lg.T[0:N_EXPERTS, :]
    eio = lax.broadcasted_iota(I32, (N_EXPERTS, tile), 0)
    vals, sels = _top4(lgt, eio)
    es = [jnp.exp(v - vals[0]) for v in vals]
    inv = 1.0 / (es[0] + es[1] + es[2] + es[3])
    onehot = jnp.where(sels[0] | sels[1] | sels[2] | sels[3], 1.0, 0.0)
    prefix = jnp.dot(onehot.astype(BF16), tri_ref[...], preferred_element_type=F32)
    n_tok = jnp.sum(onehot, axis=1, keepdims=True)
    n_seg = jnp.maximum(jnp.floor((n_tok + (SEG_ALIGN - 1)) * (1.0 / SEG_ALIGN)), 1.0) * SEG_ALIGN
    off = jnp.dot(low_ref[...], jnp.broadcast_to(n_seg, (N_EXPERTS, LANES)).astype(BF16),
                  preferred_element_type=F32)
    pos = off[:, 0:1] + prefix
    for kk in range(TOP_K):
        gate_ref[kk:kk + 1, cols] = es[kk] * inv
        lpos_ref[kk:kk + 1, cols] = jnp.sum(jnp.where(sels[kk], pos, 0.0), axis=0,
                                            keepdims=True).astype(I32)
    lane = lax.broadcasted_iota(I32, (N_EXPERTS, LANES), 1)
    seg_ref[...] = jnp.where(lane == 0, n_seg, jnp.where(lane == 1, run_ref[...],
                                                         jnp.where(lane == 2, off, 0.0)))
    run_ref[...] = run_ref[...] + n_seg


def _mix_body(q_ref, k_ref, v_ref, up_ref, um_ref, un_ref, x_ref,
              dww_ref, dwb_ref, lng_ref, lnb_ref, wpw_ref, cog_ref, aog_ref, wout_ref,
              g2_ref, rw_ref, rb_ref, tri_ref, low_ref,
              x1_ref, h2_ref, lpos_ref, gate_ref, seg_ref, cnt_ref,
              attn_ref, wpad_ref, shift_ref, conv_ref, run_ref):
    b = pl.program_id(0)
    i = pl.program_id(1)
    n_i = pl.num_programs(1)
    rows = um_ref.shape[0]
    tile = tri_ref.shape[0]

    @pl.when((b == 0) & (i == 0))
    def _():
        run_ref[...] = jnp.zeros_like(run_ref)

    wpad_ref[0:HALO, :] = jnp.where(i > 0, up_ref[...], 0.0)
    wpad_ref[HALO:HALO + rows, :] = um_ref[...]
    wpad_ref[HALO + rows:, :] = jnp.where(i < n_i - 1, un_ref[...], 0.0)
    span = rows + 2 * HALO - SUBLANES
    for r in range(1, SUBLANES):
        shift_ref[r - 1] = wpad_ref[r:r + span, :]
    part = min(CONV_ROWS, rows)
    for c in range(D_CONV // LANES):
        cs = slice(c * LANES, (c + 1) * LANES)
        for r0 in range(0, rows, part):
            acc = jnp.zeros((part, LANES), F32)
            for j in range(CONV_K):
                off = HALO - CONV_PAD + j
                r, base = off % SUBLANES, r0 + off - off % SUBLANES
                tap = (wpad_ref[base:base + part, cs] if r == 0
                       else shift_ref[r - 1, base:base + part, cs])
                acc = acc + tap * dww_ref[j:j + 1, cs]
            conv_ref[r0:r0 + part, cs] = acc + dwb_ref[:, cs]
    _attend(q_ref, k_ref, v_ref, attn_ref)
    y = conv_ref[...]
    mu = jnp.mean(y, axis=-1, keepdims=True)
    yc = y - mu
    var = jnp.mean(yc * yc, axis=-1, keepdims=True)
    y = yc * lax.rsqrt(var + EPS) * lng_ref[...] + lnb_ref[...]
    y = y * jax.nn.sigmoid(y)
    conv_out = jnp.dot(y.astype(BF16), wpw_ref[...], preferred_element_type=F32)

    cn = _rms(conv_out, cog_ref[...]).astype(BF16)
    an = _rms(attn_ref[...].astype(F32), aog_ref[...]).astype(BF16)
    mix = (jnp.dot(cn, wout_ref[0:D_CONV, :], preferred_element_type=F32)
           + jnp.dot(an, wout_ref[D_CONV:, :], preferred_element_type=F32))
    x1 = x_ref[...] + mix
    x1_ref[...] = x1
    h2 = _rms(x1, g2_ref[...])
    hi, lo = _split_bf16(h2)
    h2_ref[...] = hi

    hi_part = jnp.dot(hi, rw_ref[...], preferred_element_type=F32)
    lg = (hi_part[:, :LANES] + hi_part[:, LANES:]
          + jnp.dot(lo, rw_ref[:, 0:LANES], preferred_element_type=F32)) + rb_ref[...]
    for t in range(rows // tile):
        cols = slice(t * tile, (t + 1) * tile)
        _route(lg[cols, :], tri_ref, low_ref, run_ref, lpos_ref, gate_ref, seg_ref.at[t], cols)
    cnt_ref[...] = run_ref[...]


def _mix(q, k, v, u, x, p):
    B, S, _ = x.shape
    rows = min(MIX_ROWS, S)
    tile = p["tri"].shape[0]
    n_i = S // rows
    T = B * S
    hb = rows // HALO
    n_halo = S // HALO
    full = lambda a: pl.BlockSpec(a.shape, lambda b, i: (0,) * a.ndim)
    row_blk = lambda w: pl.BlockSpec((None, rows, w), lambda b, i: (b, i, 0))
    seq_blk = lambda w: pl.BlockSpec((None, S, w), lambda b, i: (b, 0, 0))
    tok_blk = lambda r: pl.BlockSpec((r, rows), lambda b, i: (0, b * n_i + i))
    weights = [p["dw_w"], p["dw_b"], p["ln_g"], p["ln_b"], p["w_pw2"], p["conv_out_g"],
               p["attn_out_g"], p["w_out"], p["norm2_g"], p["rw"], p["rb"],
               p["tri"], p["low"]]
    return pl.pallas_call(
        _mix_body,
        grid=(B, n_i),
        in_specs=[row_blk(D_ATTN), seq_blk(D_KV), seq_blk(N_KV_HEADS * LANES),
                  pl.BlockSpec((None, HALO, D_CONV), lambda b, i: (b, jnp.maximum(i * hb - 1, 0), 0)),
                  row_blk(D_CONV),
                  pl.BlockSpec((None, HALO, D_CONV),
                               lambda b, i: (b, jnp.minimum((i + 1) * hb, n_halo - 1), 0)),
                  row_blk(D_MODEL)] + [full(w) for w in weights],
        out_specs=[row_blk(D_MODEL),
                   pl.BlockSpec((rows, D_MODEL), lambda b, i: (b * n_i + i, 0)),
                   tok_blk(TOP_K), tok_blk(TOP_K),
                   pl.BlockSpec((rows // tile, N_EXPERTS, LANES), lambda b, i: (b * n_i + i, 0, 0)),
                   pl.BlockSpec((N_EXPERTS, LANES), lambda b, i: (0, 0))],
        out_shape=[jax.ShapeDtypeStruct((B, S, D_MODEL), F32),
                   jax.ShapeDtypeStruct((T, D_MODEL), BF16),
                   jax.ShapeDtypeStruct((TOP_K, T), I32),
                   jax.ShapeDtypeStruct((TOP_K, T), F32),
                   jax.ShapeDtypeStruct((T // tile, N_EXPERTS, LANES), F32),
                   jax.ShapeDtypeStruct((N_EXPERTS, LANES), F32)],
        scratch_shapes=[pltpu.VMEM((rows, D_ATTN), BF16),
                        pltpu.VMEM((rows + 2 * HALO, D_CONV), F32),
                        pltpu.VMEM((SUBLANES - 1, rows + 2 * HALO - SUBLANES, D_CONV), F32),
                        pltpu.VMEM((rows, D_CONV), F32),
                        pltpu.VMEM((N_EXPERTS, LANES), F32)],
        compiler_params=_cparams(2),
        name="mix",
    )(q, k, v, u, u, u, x, *weights)


def _segments(n_ref, tile, fn, maybe_empty=False):
    for e in range(N_EXPERTS):
        n = n_ref[tile * N_EXPERTS + e]
        if maybe_empty:
            pl.when(n > 0)(functools.partial(fn, e, pl.multiple_of(n, SEG_ALIGN)))
        else:
            fn(e, pl.multiple_of(n, SEG_ALIGN))


def _tile_rows(n_ref, off_ref, tile):
    last = tile * N_EXPERTS + N_EXPERTS - 1
    return pl.multiple_of(off_ref[last] + n_ref[last], SEG_ALIGN)


def _slot_matrix(lpos_ref, vals, n_slots):
    tile = lpos_ref.shape[1]
    sio = lax.broadcasted_iota(I32, (n_slots, tile), 0)
    m = jnp.zeros((n_slots, tile), F32)
    for kk in range(TOP_K):
        v = 1.0 if vals is None else vals[kk:kk + 1, :]
        m = jnp.where(sio == lpos_ref[kk:kk + 1, :], v, m)
    return m


def _dispatch_body(n_ref, off_ref, dst_ref, tlo_ref, tn_ref,
                   lpos_ref, h2_ref, xs_ref, buf_ref, zero_ref, sem, zsem):
    j = pl.program_id(0)
    n_j = pl.num_programs(0)
    slot = j % 2
    n_slots = buf_ref.shape[1]

    def seg_copy(tile, sl, e, n):
        src = buf_ref.at[sl, pl.ds(pl.multiple_of(off_ref[tile * N_EXPERTS + e], SEG_ALIGN), n), :]
        dst = xs_ref.at[pl.ds(pl.multiple_of(dst_ref[tile * N_EXPERTS + e], SEG_ALIGN), n), :]
        return pltpu.make_async_copy(src, dst, sem.at[sl])

    def tail_copy(e, n):
        return pltpu.make_async_copy(
            zero_ref.at[pl.ds(0, n), :],
            xs_ref.at[pl.ds(pl.multiple_of(tlo_ref[e], SEG_ALIGN), n), :], zsem)

    @pl.when(j == 0)
    def _():
        zero_ref[...] = jnp.zeros_like(zero_ref)
        _segments(tn_ref, 0, lambda e, n: tail_copy(e, n).start(), maybe_empty=True)
        _segments(tn_ref, 0, lambda e, n: tail_copy(e, n).wait(), maybe_empty=True)

    def wait_tile(tile, sl):
        total = _tile_rows(n_ref, off_ref, tile)
        pltpu.make_async_copy(buf_ref.at[sl, pl.ds(0, total), :], xs_ref.at[pl.ds(0, total), :],
                              sem.at[sl]).wait()

    @pl.when(j >= 2)
    def _():
        wait_tile(j - 2, slot)

    sel = _slot_matrix(lpos_ref, None, n_slots).astype(BF16)
    rows = jnp.dot(sel, h2_ref[...], preferred_element_type=F32)
    buf_ref[slot] = _pack_rows(rows)
    _segments(n_ref, j, lambda e, n: seg_copy(j, slot, e, n).start())

    @pl.when(j == n_j - 1)
    def _():
        @pl.when(j >= 1)
        def _():
            wait_tile(j - 1, 1 - slot)
        wait_tile(j, slot)


def _dispatch(seg_n, seg_off, seg_dst, tail_lo, tail_n, lpos, h2, n_rows):
    tile = min(ROUTE_ROWS, h2.shape[0])
    n_t = h2.shape[0] // tile
    n_slots = TOP_K * tile + N_EXPERTS * SEG_ALIGN
    return pl.pallas_call(
        _dispatch_body,
        grid_spec=pltpu.PrefetchScalarGridSpec(
            num_scalar_prefetch=5,
            grid=(n_t,),
            in_specs=[pl.BlockSpec((TOP_K, tile), lambda j, *_: (0, j)),
                      pl.BlockSpec((tile, D_MODEL), lambda j, *_: (j, 0))],
            out_specs=pl.BlockSpec(memory_space=pl.ANY),
            scratch_shapes=[pltpu.VMEM((2, n_slots, D_HALF), U32),
                            pltpu.VMEM((EXPERT_ROWS, D_HALF), U32),
                            pltpu.SemaphoreType.DMA((2,)), pltpu.SemaphoreType.DMA],
        ),
        out_shape=jax.ShapeDtypeStruct((n_rows, D_HALF), U32),
        compiler_params=_cparams(1),
        name="dispatch",
    )(seg_n, seg_off, seg_dst, tail_lo, tail_n, lpos, h2)


def _combine_body(n_ref, off_ref, dst_ref, lpos_ref, gate_ref, x1_ref, fg_ref, ys_ref,
                  y_ref, buf_ref, sem):
    j = pl.program_id(0)
    n_j = pl.num_programs(0)
    slot = j % 2
    n_slots = buf_ref.shape[1]

    def seg_copy(tile, sl, e, n):
        src = ys_ref.at[pl.ds(pl.multiple_of(dst_ref[tile * N_EXPERTS + e], SEG_ALIGN), n), :]
        dst = buf_ref.at[sl, pl.ds(pl.multiple_of(off_ref[tile * N_EXPERTS + e], SEG_ALIGN), n), :]
        return pltpu.make_async_copy(src, dst, sem.at[sl])

    @pl.when(j == 0)
    def _():
        buf_ref[...] = jnp.zeros_like(buf_ref)
        _segments(n_ref, 0, lambda e, n: seg_copy(0, 0, e, n).start())

    def wait_tile(tile, sl):
        total = _tile_rows(n_ref, off_ref, tile)
        pltpu.make_async_copy(ys_ref.at[pl.ds(0, total), :], buf_ref.at[sl, pl.ds(0, total), :],
                              sem.at[sl]).wait()

    nxt = jnp.minimum(j + 1, n_j - 1)
    _segments(n_ref, nxt, lambda e, n: seg_copy(nxt, 1 - slot, e, n).start())
    w = _slot_matrix(lpos_ref, gate_ref, n_slots).T.astype(BF16)
    wait_tile(j, slot)
    left, right = _unpack_rows(buf_ref[slot])
    moe = jnp.concatenate(
        [jnp.dot(w, half, preferred_element_type=F32) for half in (left, right)], axis=1)
    y_ref[...] = _rms(x1_ref[...] + moe, fg_ref[...])

    @pl.when(j == n_j - 1)
    def _():
        wait_tile(nxt, 1 - slot)


def _combine(seg_n, seg_off, seg_dst, lpos, gates, x1, final_g, ys):
    T = x1.shape[0]
    tile = min(ROUTE_ROWS, T)
    n_t = T // tile
    n_slots = TOP_K * tile + N_EXPERTS * SEG_ALIGN
    return pl.pallas_call(
        _combine_body,
        grid_spec=pltpu.PrefetchScalarGridSpec(
            num_scalar_prefetch=3,
            grid=(n_t,),
            in_specs=[pl.BlockSpec((TOP_K, tile), lambda j, *_: (0, j)),
                      pl.BlockSpec((TOP_K, tile), lambda j, *_: (0, j)),
                      pl.BlockSpec((tile, D_MODEL), lambda j, *_: (j, 0)),
                      pl.BlockSpec((1, D_MODEL), lambda j, *_: (0, 0)),
                      pl.BlockSpec(memory_space=pl.ANY)],
            out_specs=pl.BlockSpec((tile, D_MODEL), lambda j, *_: (j, 0)),
            scratch_shapes=[pltpu.VMEM((2, n_slots, D_HALF), U32), pltpu.SemaphoreType.DMA((2,))],
        ),
        out_shape=jax.ShapeDtypeStruct((T, D_MODEL), F32),
        compiler_params=_cparams(1),
        name="combine",
    )(seg_n, seg_off, seg_dst, lpos, gates, x1, final_g, ys)


def _expert_body(be_ref, nact_ref, xs_ref, wgu_ref, bgu_ref, wdn_ref, bdn_ref, ys_ref):
    @pl.when(pl.program_id(0) < nact_ref[0])
    def _():
        left, right = _unpack_rows(xs_ref[...])
        hgu = (jnp.dot(left, wgu_ref[0:D_HALF, :], preferred_element_type=F32)
               + jnp.dot(right, wgu_ref[D_HALF:, :], preferred_element_type=F32)) + bgu_ref[...]
        glu = jnp.minimum(hgu[:, :D_FF], SWIGLU_LIMIT)
        lin = jnp.clip(hgu[:, D_FF:], -SWIGLU_LIMIT, SWIGLU_LIMIT)
        act = glu * jax.nn.sigmoid(SWIGLU_ALPHA * glu) * (lin + 1.0)
        y = jnp.dot(act.astype(BF16), wdn_ref[...], preferred_element_type=F32) + bdn_ref[...]
        ys_ref[...] = _pack_rows(y.astype(BF16).astype(F32))


def _experts(block_expert, n_active, xs, w_gu, b_gu, w_dn, b_dn):
    n_rows = xs.shape[0]
    n_blocks = n_rows // EXPERT_ROWS
    blk = lambda i, be, na: (jnp.minimum(i, na[0] - 1), 0)
    per_e = lambda i, be, na: (be[i], 0, 0)
    return pl.pallas_call(
        _expert_body,
        grid_spec=pltpu.PrefetchScalarGridSpec(
            num_scalar_prefetch=2,
            grid=(n_blocks,),
            in_specs=[pl.BlockSpec((EXPERT_ROWS, D_HALF), blk),
                      pl.BlockSpec((None, D_MODEL, 2 * D_FF), per_e),
                      pl.BlockSpec((None, 1, 2 * D_FF), per_e),
                      pl.BlockSpec((None, D_FF, D_MODEL), per_e),
                      pl.BlockSpec((None, 1, D_MODEL), per_e)],
            out_specs=pl.BlockSpec((EXPERT_ROWS, D_HALF), blk),
        ),
        out_shape=jax.ShapeDtypeStruct((n_rows, D_HALF), U32),
        compiler_params=_cparams(1),
        name="experts",
    )(block_expert, n_active, xs, w_gu, b_gu, w_dn, b_dn)


def _rope_tables(seq_len):
    t = np.arange(seq_len)
    inv = 1.0 / (ROPE_THETA ** (np.arange(0, AXIS_DIM, 2, dtype=np.float32) / AXIS_DIM))
    inv = inv.astype(np.float32)
    ang_r = (t // GRID_W).astype(np.float32)[:, None] * inv[None, :]
    ang_c = (t % GRID_W).astype(np.float32)[:, None] * inv[None, :]
    ang = np.concatenate([ang_r, ang_r, ang_c, ang_c], axis=-1).astype(np.float32)
    cos = np.cos(ang).astype(np.float32)
    sin = np.sin(ang).astype(np.float32)
    sign = np.where((np.arange(HEAD_DIM) % AXIS_DIM) < HALF_AXIS, -1.0, 1.0).astype(np.float32)
    reps = LANES // HEAD_DIM
    return jnp.asarray(np.tile(cos, (1, reps))), jnp.asarray(np.tile(sin * sign, (1, reps)))


def _prepare(norm1_g, w_in, q_norm_g, k_norm_g, conv_dw_w, conv_dw_b, conv_ln_g, conv_ln_b, w_pw2,
             attn_out_g, conv_out_g, w_out, norm2_g, router_w, router_b, w_gate_up, b_gate_up,
             w_down, b_down, final_g):
    row = lambda a: a.reshape(1, -1).astype(F32)
    qk_gain = jnp.concatenate([jnp.tile(q_norm_g[0] * (math.log2(math.e) / math.sqrt(HEAD_DIM)), N_Q_HEADS),
                               jnp.tile(k_norm_g[0], N_KV_HEADS)]).reshape(1, D_QK)
    head = np.arange(LANES) // HEAD_DIM
    hsum = jnp.asarray(np.tile((head[:, None] == head[None, :]).astype(np.float32), (2, 1)), dtype=BF16)
    rw = jnp.pad(router_w[0].astype(F32), ((0, 0), (0, LANES - N_EXPERTS)))
    rw_hi = rw.astype(BF16)
    rw_lo = (rw - rw_hi.astype(F32)).astype(BF16)
    rb = jnp.pad(router_b[0].astype(F32), (0, LANES - N_EXPERTS)).reshape(1, LANES)
    low = np.tril(np.ones((N_EXPERTS, N_EXPERTS), np.float32), k=-1)
    bgu = b_gate_up[0]
    return dict(
        norm1_g=row(norm1_g[0]), w_in=w_in[0].astype(BF16), qk_gain=qk_gain, hsum=hsum,
        dw_w=conv_dw_w[0].astype(F32), dw_b=row(conv_dw_b[0]), ln_g=row(conv_ln_g[0]),
        ln_b=row(conv_ln_b[0]), w_pw2=w_pw2[0].astype(BF16), conv_out_g=row(conv_out_g[0]),
        attn_out_g=row(attn_out_g[0]), w_out=w_out[0].astype(BF16), norm2_g=row(norm2_g[0]),
        rw=jnp.concatenate([rw_hi, rw_lo], axis=1), rb=rb, low=jnp.asarray(low, dtype=BF16),
        w_gu=_wprep(w_gate_up[0]),
        b_gu=jnp.concatenate([bgu[:, 0::2].reshape(N_EXPERTS, -1, FF_CHUNK),
                              bgu[:, 1::2].reshape(N_EXPERTS, -1, FF_CHUNK)],
                             axis=-1).reshape(N_EXPERTS, 1, 2 * D_FF),
        w_dn=w_down[0].astype(BF16), b_dn=b_down[0].reshape(N_EXPERTS, 1, D_MODEL),
        final_g=row(final_g),
    )


def _trunk(x, p):
    B, S, _ = x.shape
    T = B * S
    cos, sin = _rope_tables(S)
    q, k, v, u = _pre(x, p["norm1_g"], p["w_in"], p["qk_gain"], cos, sin, p["hsum"])
    tile = min(ROUTE_ROWS, S)
    tri = jnp.asarray(np.triu(np.ones((tile, tile), np.float32), k=1), dtype=BF16)
    x1, h2, lpos, gates, seg, cnt = _mix(q, k, v, u, x, dict(p, tri=tri))

    n_t = T // tile
    seg = seg[:, :, 0:3].astype(I32)
    seg_n, seg_base, seg_off = seg[:, :, 0], seg[:, :, 1], seg[:, :, 2]
    counts = cnt[:, 0].astype(I32)
    padded = ((counts + EXPERT_ROWS - 1) // EXPERT_ROWS) * EXPERT_ROWS
    pend = jnp.cumsum(padded)
    pstart = pend - padded
    seg_dst = pstart[None, :] + seg_base
    n_blocks = (T * TOP_K + n_t * N_EXPERTS * SEG_ALIGN) // EXPERT_ROWS + N_EXPERTS
    n_active = pend[-1] // EXPERT_ROWS
    blk_start = jnp.minimum(jnp.arange(n_blocks, dtype=I32), n_active - 1) * EXPERT_ROWS
    block_expert = jnp.sum((blk_start[:, None] >= pend[None, :]).astype(I32), axis=1)
    block_expert = jnp.minimum(block_expert, N_EXPERTS - 1).astype(I32)
    flat = lambda a: a.reshape(-1).astype(I32)

    xs = _dispatch(flat(seg_n), flat(seg_off), flat(seg_dst), flat(pstart + counts),
                   flat(padded - counts), lpos, h2, n_blocks * EXPERT_ROWS)
    ys = _experts(block_expert, n_active.reshape(1).astype(I32), xs,
                  p["w_gu"], p["b_gu"], p["w_dn"], p["b_dn"])
    y = _combine(flat(seg_n), flat(seg_off), flat(seg_dst), lpos, gates,
                 x1.reshape(T, D_MODEL), p["final_g"], ys)
    return y.reshape(B, S, D_MODEL)


def kernel(x_prompt, x_sample, norm1_g, w_in, q_norm_g, k_norm_g, conv_dw_w, conv_dw_b, conv_ln_g,
           conv_ln_b, w_pw2, attn_out_g, conv_out_g, w_out, norm2_g, router_w, router_b, w_gate_up,
           b_gate_up, w_down, b_down, final_g):
    p = _prepare(norm1_g, w_in, q_norm_g, k_norm_g, conv_dw_w, conv_dw_b, conv_ln_g, conv_ln_b,
                 w_pw2, attn_out_g, conv_out_g, w_out, norm2_g, router_w, router_b, w_gate_up,
                 b_gate_up, w_down, b_down, final_g)
    return (_trunk(x_prompt, p), _trunk(x_sample, p))
```

```python
import functools
import math

import numpy as np
import jax
import jax.numpy as jnp
from jax import lax
from jax.experimental import pallas as pl
from jax.experimental.pallas import tpu as pltpu

F32 = jnp.float32
BF16 = jnp.bfloat16
I32 = jnp.int32
U32 = jnp.uint32

D_MODEL = 1024
GRID_W = 64
HEAD_DIM = 64
N_Q_HEADS = 8
N_KV_HEADS = 2
N_GROUPS_PER_KV = N_Q_HEADS // N_KV_HEADS
D_ATTN = N_Q_HEADS * HEAD_DIM
D_KV = N_KV_HEADS * HEAD_DIM
D_QK = D_ATTN + D_KV
AXIS_DIM = HEAD_DIM // 2
HALF_AXIS = AXIS_DIM // 2
ROPE_THETA = 10000.0
D_CONV = D_MODEL - D_ATTN
CONV_K = 31
CONV_PAD = (CONV_K - 1) // 2
D_IN_PROJ = D_ATTN + 2 * D_KV + 2 * D_CONV
N_EXPERTS = 32
TOP_K = 4
D_FF = D_MODEL
SWIGLU_ALPHA = 1.702
SWIGLU_LIMIT = 7.0
EPS = 1e-6

LANES = 128
SUBLANES = 8
HALO = 16
NEG_INF = float("-inf")
D_HALF = D_MODEL // 2
SEG_ALIGN = SUBLANES

PRE_ROWS = 1024
MIX_ROWS = 1024
CONV_ROWS = 256
ROUTE_ROWS = 256
EXPERT_ROWS = 512
WPREP_ROWS = 512
FF_CHUNK = D_FF

VMEM_LIMIT = 56 * 1024 * 1024


def _cparams(n_axes):
    return pltpu.CompilerParams(dimension_semantics=("arbitrary",) * n_axes,
                                vmem_limit_bytes=VMEM_LIMIT)


def _rms(x, g):
    return x * lax.rsqrt(jnp.mean(x * x, axis=-1, keepdims=True) + EPS) * g


def _split_bf16(x):
    hi = x.astype(BF16)
    lo = (x - hi.astype(F32)).astype(BF16)
    return hi, lo


def _pack_rows(x):
    bits = lax.bitcast_convert_type(x, U32)
    return (bits[:, :D_HALF] & jnp.uint32(0xFFFF0000)) | (bits[:, D_HALF:] >> 16)


def _unpack_rows(w):
    left = lax.bitcast_convert_type(w & jnp.uint32(0xFFFF0000), F32).astype(BF16)
    right = lax.bitcast_convert_type(w << 16, F32).astype(BF16)
    return left, right


def _wprep_body(w_ref, perm_ref, o_ref):
    perm = perm_ref[...]
    group = 2 * LANES
    for g in range(2 * D_FF // group):
        blk = w_ref[:, g * group:(g + 1) * group].astype(BF16)
        r = jnp.dot(blk, perm, preferred_element_type=F32)
        c, half = divmod(g * LANES, FF_CHUNK)
        o_ref[:, 2 * c * FF_CHUNK + half:2 * c * FF_CHUNK + half + LANES] = r[:, :LANES].astype(BF16)
        o_ref[:, (2 * c + 1) * FF_CHUNK + half:(2 * c + 1) * FF_CHUNK + half + LANES] = (
            r[:, LANES:].astype(BF16))


def _wprep(w_gu):
    group = 2 * LANES
    i = np.arange(group)[:, None]
    c = np.arange(group)[None, :]
    perm = np.where(c < LANES, i == 2 * c, i == 2 * (c - LANES) + 1).astype(np.float32)
    return pl.pallas_call(
        _wprep_body,
        grid=(N_EXPERTS, D_MODEL // WPREP_ROWS),
        in_specs=[pl.BlockSpec((None, WPREP_ROWS, 2 * D_FF), lambda e, r: (e, r, 0)),
                  pl.BlockSpec((group, group), lambda e, r: (0, 0))],
        out_specs=pl.BlockSpec((None, WPREP_ROWS, 2 * D_FF), lambda e, r: (e, r, 0)),
        out_shape=jax.ShapeDtypeStruct((N_EXPERTS, D_MODEL, 2 * D_FF), BF16),
        compiler_params=_cparams(2),
        name="wprep",
    )(w_gu, jnp.asarray(perm, dtype=BF16))


def _pre_body(x_ref, g1_ref, win_ref, qkg_ref, cos_ref, sin_ref, hsum_ref,
              q_ref, k_ref, v_ref, u_ref):
    h = _rms(x_ref[...], g1_ref[...]).astype(BF16)
    z = jnp.dot(h, win_ref[...], preferred_element_type=F32)
    cos = cos_ref[...]
    sin = sin_ref[...]
    lane = lax.broadcasted_iota(I32, (1, LANES), 1)
    first_half = (lane % AXIS_DIM) < HALF_AXIS
    hsum = hsum_ref[...]
    for c in range(D_QK // LANES):
        zc = z[:, c * LANES:(c + 1) * LANES]
        hi, lo = _split_bf16(zc * zc)
        ss = jnp.dot(jnp.concatenate([hi, lo], axis=1), hsum, preferred_element_type=F32)
        zn = zc * lax.rsqrt(ss * (1.0 / HEAD_DIM) + EPS) * qkg_ref[:, c * LANES:(c + 1) * LANES]
        rot = jnp.where(first_half,
                        pltpu.roll(zn, LANES - HALF_AXIS, 1),
                        pltpu.roll(zn, HALF_AXIS, 1))
        out = (zn * cos + rot * sin).astype(BF16)
        if c < D_ATTN // LANES:
            q_ref[:, c * LANES:(c + 1) * LANES] = out
        else:
            k_ref[...] = out
    zv = z[:, D_QK:D_QK + D_KV]
    ones_col = jnp.where(lane == HEAD_DIM, 1.0, 0.0)
    for kv in range(N_KV_HEADS):
        zk = zv if kv == 0 else pltpu.roll(zv, LANES - kv * HEAD_DIM, 1)
        v_ref[:, kv * LANES:(kv + 1) * LANES] = jnp.where(lane < HEAD_DIM, zk, ones_col).astype(BF16)
    a = z[:, D_QK + D_KV:D_QK + D_KV + D_CONV]
    gate = z[:, D_QK + D_KV + D_CONV:]
    u_ref[...] = a * jax.nn.sigmoid(gate)


def _pre(x, g1, w_in, qk_gain, cos, sin, hsum):
    B, S, _ = x.shape
    rows = min(PRE_ROWS, S)
    grid = (B, S // rows)
    row_blk = lambda w: pl.BlockSpec((None, rows, w), lambda b, i: (b, i, 0))
    full = lambda a: pl.BlockSpec(a.shape, lambda b, i: (0,) * a.ndim)
    return pl.pallas_call(
        _pre_body,
        grid=grid,
        in_specs=[row_blk(D_MODEL), full(g1), full(w_in), full(qk_gain),
                  pl.BlockSpec((rows, LANES), lambda b, i: (i, 0)),
                  pl.BlockSpec((rows, LANES), lambda b, i: (i, 0)),
                  full(hsum)],
        out_specs=[row_blk(D_ATTN), row_blk(D_KV), row_blk(N_KV_HEADS * LANES), row_blk(D_CONV)],
        out_shape=[jax.ShapeDtypeStruct((B, S, D_ATTN), BF16),
                   jax.ShapeDtypeStruct((B, S, D_KV), BF16),
                   jax.ShapeDtypeStruct((B, S, N_KV_HEADS * LANES), BF16),
                   jax.ShapeDtypeStruct((B, S, D_CONV), F32)],
        compiler_params=_cparams(2),
        name="pre",
    )(x, g1, w_in, qk_gain, cos, sin, hsum)


def _attend(q_ref, k_ref, v_ref, attn_ref):
    nt = (((1,), (1,)), ((), ()))
    for kv in range(N_KV_HEADS):
        kh = k_ref[:, kv * HEAD_DIM:(kv + 1) * HEAD_DIM]
        vh = v_ref[:, kv * LANES:(kv + 1) * LANES]
        for g in range(N_GROUPS_PER_KV):
            h = kv * N_GROUPS_PER_KV + g
            qh = q_ref[:, h * HEAD_DIM:(h + 1) * HEAD_DIM]
            s = lax.dot_general(qh, kh, nt, preferred_element_type=F32)
            m = jnp.max(s, axis=-1, keepdims=True)
            p = jnp.exp2(s - m).astype(BF16)
            o = jnp.dot(p, vh, preferred_element_type=F32)
            attn_ref[:, h * HEAD_DIM:(h + 1) * HEAD_DIM] = (
                o[:, :HEAD_DIM] / o[:, HEAD_DIM:HEAD_DIM + 1]).astype(BF16)


def _top4(lg, iota):
    vals, sels = [], []
    for _ in range(TOP_K):
        m = jnp.max(lg, axis=0, keepdims=True)
        idx = jnp.min(jnp.where(lg == m, iota, N_EXPERTS), axis=0, keepdims=True)
        sel = iota == idx
        vals.append(m)
        sels.append(sel)
        lg = jnp.where(sel, NEG_INF, lg)
    return vals, sels


def _route(lg, tri_ref, low_ref, run_ref, lpos_ref, gate_ref, seg_ref, cols):
    tile = lg.shape[0]
    lgt = lg.T[0:N_EXPERTS, :]
    eio = lax.broadcasted_iota(I32, (N_EXPERTS, tile), 0)
    vals, sels = _top4(lgt, eio)
    es = [jnp.exp(v - vals[0]) for v in vals]
    inv = 1.0 / (es[0] + es[1] + es[2] + es[3])
    onehot = jnp.where(sels[0] | sels[1] | sels[2] | sels[3], 1.0, 0.0)
    prefix = jnp.dot(onehot.astype(BF16), tri_ref[...], preferred_element_type=F32)
    n_tok = jnp.sum(onehot, axis=1, keepdims=True)
    n_seg = jnp.maximum(jnp.floor((n_tok + (SEG_ALIGN - 1)) * (1.0 / SEG_ALIGN)), 1.0) * SEG_ALIGN
    off = jnp.dot(low_ref[...], jnp.broadcast_to(n_seg, (N_EXPERTS, LANES)).astype(BF16),
                  preferred_element_type=F32)
    pos = off[:, 0:1] + prefix
    for kk in range(TOP_K):
        gate_ref[kk:kk + 1, cols] = es[kk] * inv
        lpos_ref[kk:kk + 1, cols] = jnp.sum(jnp.where(sels[kk], pos, 0.0), axis=0,
                                            keepdims=True).astype(I32)
    lane = lax.broadcasted_iota(I32, (N_EXPERTS, LANES), 1)
    seg_ref[...] = jnp.where(lane == 0, n_seg, jnp.where(lane == 1, run_ref[...],
                                                         jnp.where(lane == 2, off, 0.0)))
    run_ref[...] = run_ref[...] + n_seg


def _mix_body(q_ref, k_ref, v_ref, up_ref, um_ref, un_ref, x_ref,
              dww_ref, dwb_ref, lng_ref, lnb_ref, wpw_ref, cog_ref, aog_ref, wout_ref,
              g2_ref, rw_ref, rb_ref, tri_ref, low_ref,
              x1_ref, h2_ref, lpos_ref, gate_ref, seg_ref, cnt_ref,
              attn_ref, wpad_ref, shift_ref, conv_ref, run_ref):
    b = pl.program_id(0)
    i = pl.program_id(1)
    n_i = pl.num_programs(1)
    rows = um_ref.shape[0]
    tile = tri_ref.shape[0]

    @pl.when((b == 0) & (i == 0))
    def _():
        run_ref[...] = jnp.zeros_like(run_ref)

    wpad_ref[0:HALO, :] = jnp.where(i > 0, up_ref[...], 0.0)
    wpad_ref[HALO:HALO + rows, :] = um_ref[...]
    wpad_ref[HALO + rows:, :] = jnp.where(i < n_i - 1, un_ref[...], 0.0)
    span = rows + 2 * HALO - SUBLANES
    for r in range(1, SUBLANES):
        shift_ref[r - 1] = wpad_ref[r:r + span, :]
    part = min(CONV_ROWS, rows)
    for c in range(D_CONV // LANES):
        cs = slice(c * LANES, (c + 1) * LANES)
        for r0 in range(0, rows, part):
            acc = jnp.zeros((part, LANES), F32)
            for j in range(CONV_K):
                off = HALO - CONV_PAD + j
                r, base = off % SUBLANES, r0 + off - off % SUBLANES
                tap = (wpad_ref[base:base + part, cs] if r == 0
                       else shift_ref[r - 1, base:base + part, cs])
                acc = acc + tap * dww_ref[j:j + 1, cs]
            conv_ref[r0:r0 + part, cs] = acc + dwb_ref[:, cs]
    _attend(q_ref, k_ref, v_ref, attn_ref)
    y = conv_ref[...]
    mu = jnp.mean(y, axis=-1, keepdims=True)
    yc = y - mu
    var = jnp.mean(yc * yc, axis=-1, keepdims=True)
    y = yc * lax.rsqrt(var + EPS) * lng_ref[...] + lnb_ref[...]
    y = y * jax.nn.sigmoid(y)
    conv_out = jnp.dot(y.astype(BF16), wpw_ref[...], preferred_element_type=F32)

    cn = _rms(conv_out, cog_ref[...]).astype(BF16)
    an = _rms(attn_ref[...].astype(F32), aog_ref[...]).astype(BF16)
    mix = (jnp.dot(cn, wout_ref[0:D_CONV, :], preferred_element_type=F32)
           + jnp.dot(an, wout_ref[D_CONV:, :], preferred_element_type=F32))
    x1 = x_ref[...] + mix
    x1_ref[...] = x1
    h2 = _rms(x1, g2_ref[...])
    hi, lo = _split_bf16(h2)
    h2_ref[...] = hi

    hi_part = jnp.dot(hi, rw_ref[...], preferred_element_type=F32)
    lg = (hi_part[:, :LANES] + hi_part[:, LANES:]
          + jnp.dot(lo, rw_ref[:, 0:LANES], preferred_element_type=F32)) + rb_ref[...]
    for t in range(rows // tile):
        cols = slice(t * tile, (t + 1) * tile)
        _route(lg[cols, :], tri_ref, low_ref, run_ref, lpos_ref, gate_ref, seg_ref.at[t], cols)
    cnt_ref[...] = run_ref[...]


def _mix(q, k, v, u, x, p):
    B, S, _ = x.shape
    rows = min(MIX_ROWS, S)
    tile = p["tri"].shape[0]
    n_i = S // rows
    T = B * S
    hb = rows // HALO
    n_halo = S // HALO
    full = lambda a: pl.BlockSpec(a.shape, lambda b, i: (0,) * a.ndim)
    row_blk = lambda w: pl.BlockSpec((None, rows, w), lambda b, i: (b, i, 0))
    seq_blk = lambda w: pl.BlockSpec((None, S, w), lambda b, i: (b, 0, 0))
    tok_blk = lambda r: pl.BlockSpec((r, rows), lambda b, i: (0, b * n_i + i))
    weights = [p["dw_w"], p["dw_b"], p["ln_g"], p["ln_b"], p["w_pw2"], p["conv_out_g"],
               p["attn_out_g"], p["w_out"], p["norm2_g"], p["rw"], p["rb"],
               p["tri"], p["low"]]
    return pl.pallas_call(
        _mix_body,
        grid=(B, n_i),
        in_specs=[row_blk(D_ATTN), seq_blk(D_KV), seq_blk(N_KV_HEADS * LANES),
                  pl.BlockSpec((None, HALO, D_CONV), lambda b, i: (b, jnp.maximum(i * hb - 1, 0), 0)),
                  row_blk(D_CONV),
                  pl.BlockSpec((None, HALO, D_CONV),
                               lambda b, i: (b, jnp.minimum((i + 1) * hb, n_halo - 1), 0)),
                  row_blk(D_MODEL)] + [full(w) for w in weights],
        out_specs=[row_blk(D_MODEL),
                   pl.BlockSpec((rows, D_MODEL), lambda b, i: (b * n_i + i, 0)),
                   tok_blk(TOP_K), tok_blk(TOP_K),
                   pl.BlockSpec((rows // tile, N_EXPERTS, LANES), lambda b, i: (b * n_i + i, 0, 0)),
                   pl.BlockSpec((N_EXPERTS, LANES), lambda b, i: (0, 0))],
        out_shape=[jax.ShapeDtypeStruct((B, S, D_MODEL), F32),
                   jax.ShapeDtypeStruct((T, D_MODEL), BF16),
                   jax.ShapeDtypeStruct((TOP_K, T), I32),
                   jax.ShapeDtypeStruct((TOP_K, T), F32),
                   jax.ShapeDtypeStruct((T // tile, N_EXPERTS, LANES), F32),
                   jax.ShapeDtypeStruct((N_EXPERTS, LANES), F32)],
        scratch_shapes=[pltpu.VMEM((rows, D_ATTN), BF16),
                        pltpu.VMEM((rows + 2 * HALO, D_CONV), F32),
                        pltpu.VMEM((SUBLANES - 1, rows + 2 * HALO - SUBLANES, D_CONV), F32),
                        pltpu.VMEM((rows, D_CONV), F32),
                        pltpu.VMEM((N_EXPERTS, LANES), F32)],
        compiler_params=_cparams(2),
        name="mix",
    )(q, k, v, u, u, u, x, *weights)


def _segments(n_ref, tile, fn, maybe_empty=False):
    for e in range(N_EXPERTS):
        n = n_ref[tile * N_EXPERTS + e]
        if maybe_empty:
            pl.when(n > 0)(functools.partial(fn, e, pl.multiple_of(n, SEG_ALIGN)))
        else:
            fn(e, pl.multiple_of(n, SEG_ALIGN))


def _tile_rows(n_ref, off_ref, tile):
    last = tile * N_EXPERTS + N_EXPERTS - 1
    return pl.multiple_of(off_ref[last] + n_ref[last], SEG_ALIGN)


def _slot_matrix(lpos_ref, vals, n_slots):
    tile = lpos_ref.shape[1]
    sio = lax.broadcasted_iota(I32, (n_slots, tile), 0)
    m = jnp.zeros((n_slots, tile), F32)
    for kk in range(TOP_K):
        v = 1.0 if vals is None else vals[kk:kk + 1, :]
        m = jnp.where(sio == lpos_ref[kk:kk + 1, :], v, m)
    return m


def _dispatch_body(n_ref, off_ref, dst_ref, tlo_ref, tn_ref,
                   lpos_ref, h2_ref, xs_ref, buf_ref, zero_ref, sem, zsem):
    j = pl.program_id(0)
    n_j = pl.num_programs(0)
    slot = j % 2
    n_slots = buf_ref.shape[1]

    def seg_copy(tile, sl, e, n):
        src = buf_ref.at[sl, pl.ds(pl.multiple_of(off_ref[tile * N_EXPERTS + e], SEG_ALIGN), n), :]
        dst = xs_ref.at[pl.ds(pl.multiple_of(dst_ref[tile * N_EXPERTS + e], SEG_ALIGN), n), :]
        return pltpu.make_async_copy(src, dst, sem.at[sl])

    def tail_copy(e, n):
        return pltpu.make_async_copy(
            zero_ref.at[pl.ds(0, n), :],
            xs_ref.at[pl.ds(pl.multiple_of(tlo_ref[e], SEG_ALIGN), n), :], zsem)

    @pl.when(j == 0)
    def _():
        zero_ref[...] = jnp.zeros_like(zero_ref)
        _segments(tn_ref, 0, lambda e, n: tail_copy(e, n).start(), maybe_empty=True)
        _segments(tn_ref, 0, lambda e, n: tail_copy(e, n).wait(), maybe_empty=True)

    def wait_tile(tile, sl):
        total = _tile_rows(n_ref, off_ref, tile)
        pltpu.make_async_copy(buf_ref.at[sl, pl.ds(0, total), :], xs_ref.at[pl.ds(0, total), :],
                              sem.at[sl]).wait()

    @pl.when(j >= 2)
    def _():
        wait_tile(j - 2, slot)

    sel = _slot_matrix(lpos_ref, None, n_slots).astype(BF16)
    rows = jnp.dot(sel, h2_ref[...], preferred_element_type=F32)
    buf_ref[slot] = _pack_rows(rows)
    _segments(n_ref, j, lambda e, n: seg_copy(j, slot, e, n).start())

    @pl.when(j == n_j - 1)
    def _():
        @pl.when(j >= 1)
        def _():
            wait_tile(j - 1, 1 - slot)
        wait_tile(j, slot)


def _dispatch(seg_n, seg_off, seg_dst, tail_lo, tail_n, lpos, h2, n_rows):
    tile = min(ROUTE_ROWS, h2.shape[0])
    n_t = h2.shape[0] // tile
    n_slots = TOP_K * tile + N_EXPERTS * SEG_ALIGN
    return pl.pallas_call(
        _dispatch_body,
        grid_spec=pltpu.PrefetchScalarGridSpec(
            num_scalar_prefetch=5,
            grid=(n_t,),
            in_specs=[pl.BlockSpec((TOP_K, tile), lambda j, *_: (0, j)),
                      pl.BlockSpec((tile, D_MODEL), lambda j, *_: (j, 0))],
            out_specs=pl.BlockSpec(memory_space=pl.ANY),
            scratch_shapes=[pltpu.VMEM((2, n_slots, D_HALF), U32),
                            pltpu.VMEM((EXPERT_ROWS, D_HALF), U32),
                            pltpu.SemaphoreType.DMA((2,)), pltpu.SemaphoreType.DMA],
        ),
        out_shape=jax.ShapeDtypeStruct((n_rows, D_HALF), U32),
        compiler_params=_cparams(1),
        name="dispatch",
    )(seg_n, seg_off, seg_dst, tail_lo, tail_n, lpos, h2)


def _combine_body(n_ref, off_ref, dst_ref, lpos_ref, gate_ref, x1_ref, fg_ref, ys_ref,
                  y_ref, buf_ref, sem):
    j = pl.program_id(0)
    n_j = pl.num_programs(0)
    slot = j % 2
    n_slots = buf_ref.shape[1]

    def seg_copy(tile, sl, e, n):
        src = ys_ref.at[pl.ds(pl.multiple_of(dst_ref[tile * N_EXPERTS + e], SEG_ALIGN), n), :]
        dst = buf_ref.at[sl, pl.ds(pl.multiple_of(off_ref[tile * N_EXPERTS + e], SEG_ALIGN), n), :]
        return pltpu.make_async_copy(src, dst, sem.at[sl])

    @pl.when(j == 0)
    def _():
        buf_ref[...] = jnp.zeros_like(buf_ref)
        _segments(n_ref, 0, lambda e, n: seg_copy(0, 0, e, n).start())

    def wait_tile(tile, sl):
        total = _tile_rows(n_ref, off_ref, tile)
        pltpu.make_async_copy(ys_ref.at[pl.ds(0, total), :], buf_ref.at[sl, pl.ds(0, total), :],
                              sem.at[sl]).wait()

    nxt = jnp.minimum(j + 1, n_j - 1)
    _segments(n_ref, nxt, lambda e, n: seg_copy(nxt, 1 - slot, e, n).start())
    w = _slot_matrix(lpos_ref, gate_ref, n_slots).T.astype(BF16)
    wait_tile(j, slot)
    left, right = _unpack_rows(buf_ref[slot])
    moe = jnp.concatenate(
        [jnp.dot(w, half, preferred_element_type=F32) for half in (left, right)], axis=1)
    y_ref[...] = _rms(x1_ref[...] + moe, fg_ref[...])

    @pl.when(j == n_j - 1)
    def _():
        wait_tile(nxt, 1 - slot)


def _combine(seg_n, seg_off, seg_dst, lpos, gates, x1, final_g, ys):
    T = x1.shape[0]
    tile = min(ROUTE_ROWS, T)
    n_t = T // tile
    n_slots = TOP_K * tile + N_EXPERTS * SEG_ALIGN
    return pl.pallas_call(
        _combine_body,
        grid_spec=pltpu.PrefetchScalarGridSpec(
            num_scalar_prefetch=3,
            grid=(n_t,),
            in_specs=[pl.BlockSpec((TOP_K, tile), lambda j, *_: (0, j)),
                      pl.BlockSpec((TOP_K, tile), lambda j, *_: (0, j)),
                      pl.BlockSpec((tile, D_MODEL), lambda j, *_: (j, 0)),
                      pl.BlockSpec((1, D_MODEL), lambda j, *_: (0, 0)),
                      pl.BlockSpec(memory_space=pl.ANY)],
            out_specs=pl.BlockSpec((tile, D_MODEL), lambda j, *_: (j, 0)),
            scratch_shapes=[pltpu.VMEM((2, n_slots, D_HALF), U32), pltpu.SemaphoreType.DMA((2,))],
        ),
        out_shape=jax.ShapeDtypeStruct((T, D_MODEL), F32),
        compiler_params=_cparams(1),
        name="combine",
    )(seg_n, seg_off, seg_dst, lpos, gates, x1, final_g, ys)


def _expert_body(be_ref, nact_ref, xs_ref, wgu_ref, bgu_ref, wdn_ref, bdn_ref, ys_ref):
    @pl.when(pl.program_id(0) < nact_ref[0])
    def _():
        left, right = _unpack_rows(xs_ref[...])
        hgu = (jnp.dot(left, wgu_ref[0:D_HALF, :], preferred_element_type=F32)
               + jnp.dot(right, wgu_ref[D_HALF:, :], preferred_element_type=F32)) + bgu_ref[...]
        glu = jnp.minimum(hgu[:, :D_FF], SWIGLU_LIMIT)
        lin = jnp.clip(hgu[:, D_FF:], -SWIGLU_LIMIT, SWIGLU_LIMIT)
        act = glu * jax.nn.sigmoid(SWIGLU_ALPHA * glu) * (lin + 1.0)
        y = jnp.dot(act.astype(BF16), wdn_ref[...], preferred_element_type=F32) + bdn_ref[...]
        ys_ref[...] = _pack_rows(y.astype(BF16).astype(F32))


def _experts(block_expert, n_active, xs, w_gu, b_gu, w_dn, b_dn):
    n_rows = xs.shape[0]
    n_blocks = n_rows // EXPERT_ROWS
    blk = lambda i, be, na: (jnp.minimum(i, na[0] - 1), 0)
    per_e = lambda i, be, na: (be[i], 0, 0)
    return pl.pallas_call(
        _expert_body,
        grid_spec=pltpu.PrefetchScalarGridSpec(
            num_scalar_prefetch=2,
            grid=(n_blocks,),
            in_specs=[pl.BlockSpec((EXPERT_ROWS, D_HALF), blk),
                      pl.BlockSpec((None, D_MODEL, 2 * D_FF), per_e),
                      pl.BlockSpec((None, 1, 2 * D_FF), per_e),
                      pl.BlockSpec((None, D_FF, D_MODEL), per_e),
                      pl.BlockSpec((None, 1, D_MODEL), per_e)],
            out_specs=pl.BlockSpec((EXPERT_ROWS, D_HALF), blk),
        ),
        out_shape=jax.ShapeDtypeStruct((n_rows, D_HALF), U32),
        compiler_params=_cparams(1),
        name="experts",
    )(block_expert, n_active, xs, w_gu, b_gu, w_dn, b_dn)


def _rope_tables(seq_len):
    t = np.arange(seq_len)
    inv = 1.0 / (ROPE_THETA ** (np.arange(0, AXIS_DIM, 2, dtype=np.float32) / AXIS_DIM))
    inv = inv.astype(np.float32)
    ang_r = (t // GRID_W).astype(np.float32)[:, None] * inv[None, :]
    ang_c = (t % GRID_W).astype(np.float32)[:, None] * inv[None, :]
    ang = np.concatenate([ang_r, ang_r, ang_c, ang_c], axis=-1).astype(np.float32)
    cos = np.cos(ang).astype(np.float32)
    sin = np.sin(ang).astype(np.float32)
    sign = np.where((np.arange(HEAD_DIM) % AXIS_DIM) < HALF_AXIS, -1.0, 1.0).astype(np.float32)
    reps = LANES // HEAD_DIM
    return jnp.asarray(np.tile(cos, (1, reps))), jnp.asarray(np.tile(sin * sign, (1, reps)))


def _prepare(norm1_g, w_in, q_norm_g, k_norm_g, conv_dw_w, conv_dw_b, conv_ln_g, conv_ln_b, w_pw2,
             attn_out_g, conv_out_g, w_out, norm2_g, router_w, router_b, w_gate_up, b_gate_up,
             w_down, b_down, final_g):
    row = lambda a: a.reshape(1, -1).astype(F32)
    qk_gain = jnp.concatenate([jnp.tile(q_norm_g[0] * (math.log2(math.e) / math.sqrt(HEAD_DIM)), N_Q_HEADS),
                               jnp.tile(k_norm_g[0], N_KV_HEADS)]).reshape(1, D_QK)
    head = np.arange(LANES) // HEAD_DIM
    hsum = jnp.asarray(np.tile((head[:, None] == head[None, :]).astype(np.float32), (2, 1)), dtype=BF16)
    rw = jnp.pad(router_w[0].astype(F32), ((0, 0), (0, LANES - N_EXPERTS)))
    rw_hi = rw.astype(BF16)
    rw_lo = (rw - rw_hi.astype(F32)).astype(BF16)
    rb = jnp.pad(router_b[0].astype(F32), (0, LANES - N_EXPERTS)).reshape(1, LANES)
    low = np.tril(np.ones((N_EXPERTS, N_EXPERTS), np.float32), k=-1)
    bgu = b_gate_up[0]
    return dict(
        norm1_g=row(norm1_g[0]), w_in=w_in[0].astype(BF16), qk_gain=qk_gain, hsum=hsum,
        dw_w=conv_dw_w[0].astype(F32), dw_b=row(conv_dw_b[0]), ln_g=row(conv_ln_g[0]),
        ln_b=row(conv_ln_b[0]), w_pw2=w_pw2[0].astype(BF16), conv_out_g=row(conv_out_g[0]),
        attn_out_g=row(attn_out_g[0]), w_out=w_out[0].astype(BF16), norm2_g=row(norm2_g[0]),
        rw=jnp.concatenate([rw_hi, rw_lo], axis=1), rb=rb, low=jnp.asarray(low, dtype=BF16),
        w_gu=_wprep(w_gate_up[0]),
        b_gu=jnp.concatenate([bgu[:, 0::2].reshape(N_EXPERTS, -1, FF_CHUNK),
                              bgu[:, 1::2].reshape(N_EXPERTS, -1, FF_CHUNK)],
                             axis=-1).reshape(N_EXPERTS, 1, 2 * D_FF),
        w_dn=w_down[0].astype(BF16), b_dn=b_down[0].reshape(N_EXPERTS, 1, D_MODEL),
        final_g=row(final_g),
    )


def _trunk(x, p):
    B, S, _ = x.shape
    T = B * S
    cos, sin = _rope_tables(S)
    q, k, v, u = _pre(x, p["norm1_g"], p["w_in"], p["qk_gain"], cos, sin, p["hsum"])
    tile = min(ROUTE_ROWS, S)
    tri = jnp.asarray(np.triu(np.ones((tile, tile), np.float32), k=1), dtype=BF16)
    x1, h2, lpos, gates, seg, cnt = _mix(q, k, v, u, x, dict(p, tri=tri))

    n_t = T // tile
    seg = seg[:, :, 0:3].astype(I32)
    seg_n, seg_base, seg_off = seg[:, :, 0], seg[:, :, 1], seg[:, :, 2]
    counts = cnt[:, 0].astype(I32)
    padded = ((counts + EXPERT_ROWS - 1) // EXPERT_ROWS) * EXPERT_ROWS
    pend = jnp.cumsum(padded)
    pstart = pend - padded
    seg_dst = pstart[None, :] + seg_base
    n_blocks = (T * TOP_K + n_t * N_EXPERTS * SEG_ALIGN) // EXPERT_ROWS + N_EXPERTS
    n_active = pend[-1] // EXPERT_ROWS
    blk_start = jnp.minimum(jnp.arange(n_blocks, dtype=I32), n_active - 1) * EXPERT_ROWS
    block_expert = jnp.sum((blk_start[:, None] >= pend[None, :]).astype(I32), axis=1)
    block_expert = jnp.minimum(block_expert, N_EXPERTS - 1).astype(I32)
    flat = lambda a: a.reshape(-1).astype(I32)

    xs = _dispatch(flat(seg_n), flat(seg_off), flat(seg_dst), flat(pstart + counts),
                   flat(padded - counts), lpos, h2, n_blocks * EXPERT_ROWS)
    ys = _experts(block_expert, n_active.reshape(1).astype(I32), xs,
                  p["w_gu"], p["b_gu"], p["w_dn"], p["b_dn"])
    y = _combine(flat(seg_n), flat(seg_off), flat(seg_dst), lpos, gates,
                 x1.reshape(T, D_MODEL), p["final_g"], ys)
    return y.reshape(B, S, D_MODEL)


def kernel(x_prompt, x_sample, norm1_g, w_in, q_norm_g, k_norm_g, conv_dw_w, conv_dw_b, conv_ln_g,
           conv_ln_b, w_pw2, attn_out_g, conv_out_g, w_out, norm2_g, router_w, router_b, w_gate_up,
           b_gate_up, w_down, b_down, final_g):
    p = _prepare(norm1_g, w_in, q_norm_g, k_norm_g, conv_dw_w, conv_dw_b, conv_ln_g, conv_ln_b,
                 w_pw2, attn_out_g, conv_out_g, w_out, norm2_g, router_w, router_b, w_gate_up,
                 b_gate_up, w_down, b_down, final_g)
    return (_trunk(x_prompt, p), _trunk(x_sample, p))
```

```python
import functools
import math

import numpy as np
import jax
import jax.numpy as jnp
from jax import lax
from jax.experimental import pallas as pl
from jax.experimental.pallas import tpu as pltpu

F32 = jnp.float32
BF16 = jnp.bfloat16
I32 = jnp.int32
U32 = jnp.uint32

D_MODEL = 1024
GRID_W = 64
HEAD_DIM = 64
N_Q_HEADS = 8
N_KV_HEADS = 2
N_GROUPS_PER_KV = N_Q_HEADS // N_KV_HEADS
D_ATTN = N_Q_HEADS * HEAD_DIM
D_KV = N_KV_HEADS * HEAD_DIM
D_QK = D_ATTN + D_KV
AXIS_DIM = HEAD_DIM // 2
HALF_AXIS = AXIS_DIM // 2
ROPE_THETA = 10000.0
D_CONV = D_MODEL - D_ATTN
CONV_K = 31
CONV_PAD = (CONV_K - 1) // 2
D_IN_PROJ = D_ATTN + 2 * D_KV + 2 * D_CONV
N_EXPERTS = 32
TOP_K = 4
D_FF = D_MODEL
SWIGLU_ALPHA = 1.702
SWIGLU_LIMIT = 7.0
EPS = 1e-6

LANES = 128
SUBLANES = 8
HALO = 16
NEG_INF = float("-inf")
D_HALF = D_MODEL // 2
SEG_ALIGN = SUBLANES

PRE_ROWS = 1024
MIX_ROWS = 512
CONV_ROWS = 256
ROUTE_ROWS = 256
EXPERT_ROWS = 512
WPREP_ROWS = 512
FF_CHUNK = D_FF

VMEM_LIMIT = 56 * 1024 * 1024


def _cparams(n_axes):
    return pltpu.CompilerParams(dimension_semantics=("arbitrary",) * n_axes,
                                vmem_limit_bytes=VMEM_LIMIT)


def _rms(x, g):
    return x * lax.rsqrt(jnp.mean(x * x, axis=-1, keepdims=True) + EPS) * g


def _split_bf16(x):
    hi = x.astype(BF16)
    lo = (x - hi.astype(F32)).astype(BF16)
    return hi, lo


def _pack_rows(x):
    bits = lax.bitcast_convert_type(x, U32)
    return (bits[:, :D_HALF] & jnp.uint32(0xFFFF0000)) | (bits[:, D_HALF:] >> 16)


def _unpack_rows(w):
    left = lax.bitcast_convert_type(w & jnp.uint32(0xFFFF0000), F32).astype(BF16)
    right = lax.bitcast_convert_type(w << 16, F32).astype(BF16)
    return left, right


def _wprep_body(w_ref, perm_ref, o_ref):
    perm = perm_ref[...]
    group = 2 * LANES
    for g in range(2 * D_FF // group):
        blk = w_ref[:, g * group:(g + 1) * group].astype(BF16)
        r = jnp.dot(blk, perm, preferred_element_type=F32)
        c, half = divmod(g * LANES, FF_CHUNK)
        o_ref[:, 2 * c * FF_CHUNK + half:2 * c * FF_CHUNK + half + LANES] = r[:, :LANES].astype(BF16)
        o_ref[:, (2 * c + 1) * FF_CHUNK + half:(2 * c + 1) * FF_CHUNK + half + LANES] = (
            r[:, LANES:].astype(BF16))


def _wprep(w_gu):
    group = 2 * LANES
    i = np.arange(group)[:, None]
    c = np.arange(group)[None, :]
    perm = np.where(c < LANES, i == 2 * c, i == 2 * (c - LANES) + 1).astype(np.float32)
    return pl.pallas_call(
        _wprep_body,
        grid=(N_EXPERTS, D_MODEL // WPREP_ROWS),
        in_specs=[pl.BlockSpec((None, WPREP_ROWS, 2 * D_FF), lambda e, r: (e, r, 0)),
                  pl.BlockSpec((group, group), lambda e, r: (0, 0))],
        out_specs=pl.BlockSpec((None, WPREP_ROWS, 2 * D_FF), lambda e, r: (e, r, 0)),
        out_shape=jax.ShapeDtypeStruct((N_EXPERTS, D_MODEL, 2 * D_FF), BF16),
        compiler_params=_cparams(2),
        name="wprep",
    )(w_gu, jnp.asarray(perm, dtype=BF16))


def _pre_body(x_ref, g1_ref, win_ref, qkg_ref, cos_ref, sin_ref, hsum_ref,
              q_ref, k_ref, v_ref, u_ref):
    h = _rms(x_ref[...], g1_ref[...]).astype(BF16)
    z = jnp.dot(h, win_ref[...], preferred_element_type=F32)
    cos = cos_ref[...]
    sin = sin_ref[...]
    lane = lax.broadcasted_iota(I32, (1, LANES), 1)
    first_half = (lane % AXIS_DIM) < HALF_AXIS
    hsum = hsum_ref[...]
    for c in range(D_QK // LANES):
        zc = z[:, c * LANES:(c + 1) * LANES]
        hi, lo = _split_bf16(zc * zc)
        ss = jnp.dot(jnp.concatenate([hi, lo], axis=1), hsum, preferred_element_type=F32)
        zn = zc * lax.rsqrt(ss * (1.0 / HEAD_DIM) + EPS) * qkg_ref[:, c * LANES:(c + 1) * LANES]
        rot = jnp.where(first_half,
                        pltpu.roll(zn, LANES - HALF_AXIS, 1),
                        pltpu.roll(zn, HALF_AXIS, 1))
        out = (zn * cos + rot * sin).astype(BF16)
        if c < D_ATTN // LANES:
            q_ref[:, c * LANES:(c + 1) * LANES] = out
        else:
            k_ref[...] = out
    zv = z[:, D_QK:D_QK + D_KV]
    ones_col = jnp.where(lane == HEAD_DIM, 1.0, 0.0)
    for kv in range(N_KV_HEADS):
        zk = zv if kv == 0 else pltpu.roll(zv, LANES - kv * HEAD_DIM, 1)
        v_ref[:, kv * LANES:(kv + 1) * LANES] = jnp.where(lane < HEAD_DIM, zk, ones_col).astype(BF16)
    a = z[:, D_QK + D_KV:D_QK + D_KV + D_CONV]
    gate = z[:, D_QK + D_KV + D_CONV:]
    u_ref[...] = a * jax.nn.sigmoid(gate)


def _pre(x, g1, w_in, qk_gain, cos, sin, hsum):
    B, S, _ = x.shape
    rows = min(PRE_ROWS, S)
    grid = (B, S // rows)
    row_blk = lambda w: pl.BlockSpec((None, rows, w), lambda b, i: (b, i, 0))
    full = lambda a: pl.BlockSpec(a.shape, lambda b, i: (0,) * a.ndim)
    return pl.pallas_call(
        _pre_body,
        grid=grid,
        in_specs=[row_blk(D_MODEL), full(g1), full(w_in), full(qk_gain),
                  pl.BlockSpec((rows, LANES), lambda b, i: (i, 0)),
                  pl.BlockSpec((rows, LANES), lambda b, i: (i, 0)),
                  full(hsum)],
        out_specs=[row_blk(D_ATTN), row_blk(D_KV), row_blk(N_KV_HEADS * LANES), row_blk(D_CONV)],
        out_shape=[jax.ShapeDtypeStruct((B, S, D_ATTN), BF16),
                   jax.ShapeDtypeStruct((B, S, D_KV), BF16),
                   jax.ShapeDtypeStruct((B, S, N_KV_HEADS * LANES), BF16),
                   jax.ShapeDtypeStruct((B, S, D_CONV), F32)],
        compiler_params=_cparams(2),
        name="pre",
    )(x, g1, w_in, qk_gain, cos, sin, hsum)


def _attend(q_ref, k_ref, v_ref, attn_ref):
    nt = (((1,), (1,)), ((), ()))
    for kv in range(N_KV_HEADS):
        kh = k_ref[:, kv * HEAD_DIM:(kv + 1) * HEAD_DIM]
        vh = v_ref[:, kv * LANES:(kv + 1) * LANES]
        for g in range(N_GROUPS_PER_KV):
            h = kv * N_GROUPS_PER_KV + g
            qh = q_ref[:, h * HEAD_DIM:(h + 1) * HEAD_DIM]
            s = lax.dot_general(qh, kh, nt, preferred_element_type=F32)
            m = jnp.max(s, axis=-1, keepdims=True)
            p = jnp.exp2(s - m).astype(BF16)
            o = jnp.dot(p, vh, preferred_element_type=F32)
            attn_ref[:, h * HEAD_DIM:(h + 1) * HEAD_DIM] = (
                o[:, :HEAD_DIM] / o[:, HEAD_DIM:HEAD_DIM + 1]).astype(BF16)


def _top4(lg, iota):
    vals, sels = [], []
    for _ in range(TOP_K):
        m = jnp.max(lg, axis=0, keepdims=True)
        idx = jnp.min(jnp.where(lg == m, iota, N_EXPERTS), axis=0, keepdims=True)
        sel = iota == idx
        vals.append(m)
        sels.append(sel)
        lg = jnp.where(sel, NEG_INF, lg)
    return vals, sels


def _route(lg, tri_ref, low_ref, run_ref, lpos_ref, gate_ref, seg_ref, cols):
    tile = lg.shape[0]
    lgt = lg.T[0:N_EXPERTS, :]
    eio = lax.broadcasted_iota(I32, (N_EXPERTS, tile), 0)
    vals, sels = _top4(lgt, eio)
    es = [jnp.exp(v - vals[0]) for v in vals]
    inv = 1.0 / (es[0] + es[1] + es[2] + es[3])
    onehot = jnp.where(sels[0] | sels[1] | sels[2] | sels[3], 1.0, 0.0)
    prefix = jnp.dot(onehot.astype(BF16), tri_ref[...], preferred_element_type=F32)
    n_tok = jnp.sum(onehot, axis=1, keepdims=True)
    n_seg = jnp.maximum(jnp.floor((n_tok + (SEG_ALIGN - 1)) * (1.0 / SEG_ALIGN)), 1.0) * SEG_ALIGN
    off = jnp.dot(low_ref[...], jnp.broadcast_to(n_seg, (N_EXPERTS, LANES)).astype(BF16),
                  preferred_element_type=F32)
    pos = off[:, 0:1] + prefix
    for kk in range(TOP_K):
        gate_ref[kk:kk + 1, cols] = es[kk] * inv
        lpos_ref[kk:kk + 1, cols] = jnp.sum(jnp.where(sels[kk], pos, 0.0), axis=0,
                                            keepdims=True).astype(I32)
    lane = lax.broadcasted_iota(I32, (N_EXPERTS, LANES), 1)
    seg_ref[...] = jnp.where(lane == 0, n_seg, jnp.where(lane == 1, run_ref[...],
                                                         jnp.where(lane == 2, off, 0.0)))
    run_ref[...] = run_ref[...] + n_seg


def _mix_body(q_ref, k_ref, v_ref, up_ref, um_ref, un_ref, x_ref,
              dww_ref, dwb_ref, lng_ref, lnb_ref, wpw_ref, cog_ref, aog_ref, wout_ref,
              g2_ref, rw_ref, rb_ref, tri_ref, low_ref,
              x1_ref, h2_ref, lpos_ref, gate_ref, seg_ref, cnt_ref,
              attn_ref, wpad_ref, shift_ref, conv_ref, run_ref):
    b = pl.program_id(0)
    i = pl.program_id(1)
    n_i = pl.num_programs(1)
    rows = um_ref.shape[0]
    tile = tri_ref.shape[0]

    @pl.when((b == 0) & (i == 0))
    def _():
        run_ref[...] = jnp.zeros_like(run_ref)

    wpad_ref[0:HALO, :] = jnp.where(i > 0, up_ref[...], 0.0)
    wpad_ref[HALO:HALO + rows, :] = um_ref[...]
    wpad_ref[HALO + rows:, :] = jnp.where(i < n_i - 1, un_ref[...], 0.0)
    span = rows + 2 * HALO - SUBLANES
    for r in range(1, SUBLANES):
        shift_ref[r - 1] = wpad_ref[r:r + span, :]
    part = min(CONV_ROWS, rows)
    for c in range(D_CONV // LANES):
        cs = slice(c * LANES, (c + 1) * LANES)
        for r0 in range(0, rows, part):
            acc = jnp.zeros((part, LANES), F32)
            for j in range(CONV_K):
                off = HALO - CONV_PAD + j
                r, base = off % SUBLANES, r0 + off - off % SUBLANES
                tap = (wpad_ref[base:base + part, cs] if r == 0
                       else shift_ref[r - 1, base:base + part, cs])
                acc = acc + tap * dww_ref[j:j + 1, cs]
            conv_ref[r0:r0 + part, cs] = acc + dwb_ref[:, cs]
    _attend(q_ref, k_ref, v_ref, attn_ref)
    y = conv_ref[...]
    mu = jnp.mean(y, axis=-1, keepdims=True)
    yc = y - mu
    var = jnp.mean(yc * yc, axis=-1, keepdims=True)
    y = yc * lax.rsqrt(var + EPS) * lng_ref[...] + lnb_ref[...]
    y = y * jax.nn.sigmoid(y)
    conv_out = jnp.dot(y.astype(BF16), wpw_ref[...], preferred_element_type=F32)

    cn = _rms(conv_out, cog_ref[...]).astype(BF16)
    an = _rms(attn_ref[...].astype(F32), aog_ref[...]).astype(BF16)
    mix = (jnp.dot(cn, wout_ref[0:D_CONV, :], preferred_element_type=F32)
           + jnp.dot(an, wout_ref[D_CONV:, :], preferred_element_type=F32))
    x1 = x_ref[...] + mix
    x1_ref[...] = x1
    h2 = _rms(x1, g2_ref[...])
    hi, lo = _split_bf16(h2)
    h2_ref[...] = hi

    hi_part = jnp.dot(hi, rw_ref[...], preferred_element_type=F32)
    lg = (hi_part[:, :LANES] + hi_part[:, LANES:]
          + jnp.dot(lo, rw_ref[:, 0:LANES], preferred_element_type=F32)) + rb_ref[...]
    for t in range(rows // tile):
        cols = slice(t * tile, (t + 1) * tile)
        _route(lg[cols, :], tri_ref, low_ref, run_ref, lpos_ref, gate_ref, seg_ref.at[t], cols)
    cnt_ref[...] = run_ref[...]


def _mix(q, k, v, u, x, p):
    B, S, _ = x.shape
    rows = min(MIX_ROWS, S)
    tile = p["tri"].shape[0]
    n_i = S // rows
    T = B * S
    hb = rows // HALO
    n_halo = S // HALO
    full = lambda a: pl.BlockSpec(a.shape, lambda b, i: (0,) * a.ndim)
    row_blk = lambda w: pl.BlockSpec((None, rows, w), lambda b, i: (b, i, 0))
    seq_blk = lambda w: pl.BlockSpec((None, S, w), lambda b, i: (b, 0, 0))
    tok_blk = lambda r: pl.BlockSpec((r, rows), lambda b, i: (0, b * n_i + i))
    weights = [p["dw_w"], p["dw_b"], p["ln_g"], p["ln_b"], p["w_pw2"], p["conv_out_g"],
               p["attn_out_g"], p["w_out"], p["norm2_g"], p["rw"], p["rb"],
               p["tri"], p["low"]]
    return pl.pallas_call(
        _mix_body,
        grid=(B, n_i),
        in_specs=[row_blk(D_ATTN), seq_blk(D_KV), seq_blk(N_KV_HEADS * LANES),
                  pl.BlockSpec((None, HALO, D_CONV), lambda b, i: (b, jnp.maximum(i * hb - 1, 0), 0)),
                  row_blk(D_CONV),
                  pl.BlockSpec((None, HALO, D_CONV),
                               lambda b, i: (b, jnp.minimum((i + 1) * hb, n_halo - 1), 0)),
                  row_blk(D_MODEL)] + [full(w) for w in weights],
        out_specs=[row_blk(D_MODEL),
                   pl.BlockSpec((rows, D_MODEL), lambda b, i: (b * n_i + i, 0)),
                   tok_blk(TOP_K), tok_blk(TOP_K),
                   pl.BlockSpec((rows // tile, N_EXPERTS, LANES), lambda b, i: (b * n_i + i, 0, 0)),
                   pl.BlockSpec((N_EXPERTS, LANES), lambda b, i: (0, 0))],
        out_shape=[jax.ShapeDtypeStruct((B, S, D_MODEL), F32),
                   jax.ShapeDtypeStruct((T, D_MODEL), BF16),
                   jax.ShapeDtypeStruct((TOP_K, T), I32),
                   jax.ShapeDtypeStruct((TOP_K, T), F32),
                   jax.ShapeDtypeStruct((T // tile, N_EXPERTS, LANES), F32),
                   jax.ShapeDtypeStruct((N_EXPERTS, LANES), F32)],
        scratch_shapes=[pltpu.VMEM((rows, D_ATTN), BF16),
                        pltpu.VMEM((rows + 2 * HALO, D_CONV), F32),
                        pltpu.VMEM((SUBLANES - 1, rows + 2 * HALO - SUBLANES, D_CONV), F32),
                        pltpu.VMEM((rows, D_CONV), F32),
                        pltpu.VMEM((N_EXPERTS, LANES), F32)],
        compiler_params=_cparams(2),
        name="mix",
    )(q, k, v, u, u, u, x, *weights)


def _segments(n_ref, tile, fn, maybe_empty=False):
    for e in range(N_EXPERTS):
        n = n_ref[tile * N_EXPERTS + e]
        if maybe_empty:
            pl.when(n > 0)(functools.partial(fn, e, pl.multiple_of(n, SEG_ALIGN)))
        else:
            fn(e, pl.multiple_of(n, SEG_ALIGN))


def _tile_rows(n_ref, off_ref, tile):
    last = tile * N_EXPERTS + N_EXPERTS - 1
    return pl.multiple_of(off_ref[last] + n_ref[last], SEG_ALIGN)


def _slot_matrix(lpos_ref, vals, n_slots):
    tile = lpos_ref.shape[1]
    sio = lax.broadcasted_iota(I32, (n_slots, tile), 0)
    m = jnp.zeros((n_slots, tile), F32)
    for kk in range(TOP_K):
        v = 1.0 if vals is None else vals[kk:kk + 1, :]
        m = jnp.where(sio == lpos_ref[kk:kk + 1, :], v, m)
    return m


def _dispatch_body(n_ref, off_ref, dst_ref, tlo_ref, tn_ref, nact_ref,
                   lpos_ref, h2_ref, xs_ref, buf_ref, zero_ref, sem, zsem):
    j = pl.program_id(0)
    n_j = pl.num_programs(0)
    slot = j % 2
    n_slots = buf_ref.shape[1]

    def seg_copy(tile, sl, e, n):
        src = buf_ref.at[sl, pl.ds(pl.multiple_of(off_ref[tile * N_EXPERTS + e], SEG_ALIGN), n), :]
        dst = xs_ref.at[pl.ds(pl.multiple_of(dst_ref[tile * N_EXPERTS + e], SEG_ALIGN), n), :]
        return pltpu.make_async_copy(src, dst, sem.at[sl])

    def tail_copy(e, n):
        return pltpu.make_async_copy(
            zero_ref.at[pl.ds(0, n), :],
            xs_ref.at[pl.ds(pl.multiple_of(tlo_ref[e], SEG_ALIGN), n), :], zsem)

    def unused_block_copy(b):
        rows = zero_ref.shape[0]
        return pltpu.make_async_copy(
            zero_ref, xs_ref.at[pl.ds(pl.multiple_of(b * rows, rows), rows), :], zsem)

    @pl.when(j == 0)
    def _():
        zero_ref[...] = jnp.zeros_like(zero_ref)
        _segments(tn_ref, 0, lambda e, n: tail_copy(e, n).start(), maybe_empty=True)
        _segments(tn_ref, 0, lambda e, n: tail_copy(e, n).wait(), maybe_empty=True)
        n_blocks = xs_ref.shape[0] // zero_ref.shape[0]

        def start(b, c):
            unused_block_copy(b).start()
            return c

        def wait(b, c):
            unused_block_copy(b).wait()
            return c

        lax.fori_loop(nact_ref[0], n_blocks, start, 0)
        lax.fori_loop(nact_ref[0], n_blocks, wait, 0)

    def wait_tile(tile, sl):
        total = _tile_rows(n_ref, off_ref, tile)
        pltpu.make_async_copy(buf_ref.at[sl, pl.ds(0, total), :], xs_ref.at[pl.ds(0, total), :],
                              sem.at[sl]).wait()

    @pl.when(j >= 2)
    def _():
        wait_tile(j - 2, slot)

    sel = _slot_matrix(lpos_ref, None, n_slots).astype(BF16)
    rows = jnp.dot(sel, h2_ref[...], preferred_element_type=F32)
    buf_ref[slot] = _pack_rows(rows)
    _segments(n_ref, j, lambda e, n: seg_copy(j, slot, e, n).start())

    @pl.when(j == n_j - 1)
    def _():
        @pl.when(j >= 1)
        def _():
            wait_tile(j - 1, 1 - slot)
        wait_tile(j, slot)


def _dispatch(seg_n, seg_off, seg_dst, tail_lo, tail_n, n_active, lpos, h2, n_rows):
    tile = min(ROUTE_ROWS, h2.shape[0])
    n_t = h2.shape[0] // tile
    n_slots = TOP_K * tile + N_EXPERTS * SEG_ALIGN
    return pl.pallas_call(
        _dispatch_body,
        grid_spec=pltpu.PrefetchScalarGridSpec(
            num_scalar_prefetch=6,
            grid=(n_t,),
            in_specs=[pl.BlockSpec((TOP_K, tile), lambda j, *_: (0, j)),
                      pl.BlockSpec((tile, D_MODEL), lambda j, *_: (j, 0))],
            out_specs=pl.BlockSpec(memory_space=pl.ANY),
            scratch_shapes=[pltpu.VMEM((2, n_slots, D_HALF), U32),
                            pltpu.VMEM((EXPERT_ROWS, D_HALF), U32),
                            pltpu.SemaphoreType.DMA((2,)), pltpu.SemaphoreType.DMA],
        ),
        out_shape=jax.ShapeDtypeStruct((n_rows, D_HALF), U32),
        compiler_params=_cparams(1),
        name="dispatch",
    )(seg_n, seg_off, seg_dst, tail_lo, tail_n, n_active, lpos, h2)


def _combine_body(n_ref, off_ref, dst_ref, lpos_ref, gate_ref, x1_ref, fg_ref, ys_ref,
                  y_ref, buf_ref, sem):
    j = pl.program_id(0)
    n_j = pl.num_programs(0)
    slot = j % 2
    n_slots = buf_ref.shape[1]

    def seg_copy(tile, sl, e, n):
        src = ys_ref.at[pl.ds(pl.multiple_of(dst_ref[tile * N_EXPERTS + e], SEG_ALIGN), n), :]
        dst = buf_ref.at[sl, pl.ds(pl.multiple_of(off_ref[tile * N_EXPERTS + e], SEG_ALIGN), n), :]
        return pltpu.make_async_copy(src, dst, sem.at[sl])

    @pl.when(j == 0)
    def _():
        buf_ref[...] = jnp.zeros_like(buf_ref)
        _segments(n_ref, 0, lambda e, n: seg_copy(0, 0, e, n).start())

    def wait_tile(tile, sl):
        total = _tile_rows(n_ref, off_ref, tile)
        pltpu.make_async_copy(ys_ref.at[pl.ds(0, total), :], buf_ref.at[sl, pl.ds(0, total), :],
                              sem.at[sl]).wait()

    nxt = jnp.minimum(j + 1, n_j - 1)
    _segments(n_ref, nxt, lambda e, n: seg_copy(nxt, 1 - slot, e, n).start())
    w = _slot_matrix(lpos_ref, gate_ref, n_slots).T.astype(BF16)
    wait_tile(j, slot)
    left, right = _unpack_rows(buf_ref[slot])
    moe = jnp.concatenate(
        [jnp.dot(w, half, preferred_element_type=F32) for half in (left, right)], axis=1)
    y_ref[...] = _rms(x1_ref[...] + moe, fg_ref[...])

    @pl.when(j == n_j - 1)
    def _():
        wait_tile(nxt, 1 - slot)


def _combine(seg_n, seg_off, seg_dst, lpos, gates, x1, final_g, ys):
    T = x1.shape[0]
    tile = min(ROUTE_ROWS, T)
    n_t = T // tile
    n_slots = TOP_K * tile + N_EXPERTS * SEG_ALIGN
    return pl.pallas_call(
        _combine_body,
        grid_spec=pltpu.PrefetchScalarGridSpec(
            num_scalar_prefetch=3,
            grid=(n_t,),
            in_specs=[pl.BlockSpec((TOP_K, tile), lambda j, *_: (0, j)),
                      pl.BlockSpec((TOP_K, tile), lambda j, *_: (0, j)),
                      pl.BlockSpec((tile, D_MODEL), lambda j, *_: (j, 0)),
                      pl.BlockSpec((1, D_MODEL), lambda j, *_: (0, 0)),
                      pl.BlockSpec(memory_space=pl.ANY)],
            out_specs=pl.BlockSpec((tile, D_MODEL), lambda j, *_: (j, 0)),
            scratch_shapes=[pltpu.VMEM((2, n_slots, D_HALF), U32), pltpu.SemaphoreType.DMA((2,))],
        ),
        out_shape=jax.ShapeDtypeStruct((T, D_MODEL), F32),
        compiler_params=_cparams(1),
        name="combine",
    )(seg_n, seg_off, seg_dst, lpos, gates, x1, final_g, ys)


def _expert_body(be_ref, nact_ref, xs_ref, wgu_ref, bgu_ref, wdn_ref, bdn_ref, ys_ref):
    active = pl.program_id(0) < nact_ref[0]

    @pl.when(jnp.logical_not(active))
    def _():
        ys_ref[...] = jnp.zeros_like(ys_ref)

    @pl.when(active)
    def _():
        left, right = _unpack_rows(xs_ref[...])
        hgu = (jnp.dot(left, wgu_ref[0:D_HALF, :], preferred_element_type=F32)
               + jnp.dot(right, wgu_ref[D_HALF:, :], preferred_element_type=F32)) + bgu_ref[...]
        glu = jnp.minimum(hgu[:, :D_FF], SWIGLU_LIMIT)
        lin = jnp.clip(hgu[:, D_FF:], -SWIGLU_LIMIT, SWIGLU_LIMIT)
        act = glu * jax.nn.sigmoid(SWIGLU_ALPHA * glu) * (lin + 1.0)
        y = jnp.dot(act.astype(BF16), wdn_ref[...], preferred_element_type=F32) + bdn_ref[...]
        ys_ref[...] = _pack_rows(y.astype(BF16).astype(F32))


def _experts(block_expert, n_active, xs, w_gu, b_gu, w_dn, b_dn):
    n_rows = xs.shape[0]
    n_blocks = n_rows // EXPERT_ROWS
    blk = lambda i, be, na: (jnp.minimum(i, na[0] - 1), 0)
    per_e = lambda i, be, na: (be[i], 0, 0)
    return pl.pallas_call(
        _expert_body,
        grid_spec=pltpu.PrefetchScalarGridSpec(
            num_scalar_prefetch=2,
            grid=(n_blocks,),
            in_specs=[pl.BlockSpec((EXPERT_ROWS, D_HALF), blk),
                      pl.BlockSpec((None, D_MODEL, 2 * D_FF), per_e),
                      pl.BlockSpec((None, 1, 2 * D_FF), per_e),
                      pl.BlockSpec((None, D_FF, D_MODEL), per_e),
                      pl.BlockSpec((None, 1, D_MODEL), per_e)],
            out_specs=pl.BlockSpec((EXPERT_ROWS, D_HALF), lambda i, be, na: (i, 0)),
        ),
        out_shape=jax.ShapeDtypeStruct((n_rows, D_HALF), U32),
        compiler_params=_cparams(1),
        name="experts",
    )(block_expert, n_active, xs, w_gu, b_gu, w_dn, b_dn)


def _rope_tables(seq_len):
    t = np.arange(seq_len)
    inv = 1.0 / (ROPE_THETA ** (np.arange(0, AXIS_DIM, 2, dtype=np.float32) / AXIS_DIM))
    inv = inv.astype(np.float32)
    ang_r = (t // GRID_W).astype(np.float32)[:, None] * inv[None, :]
    ang_c = (t % GRID_W).astype(np.float32)[:, None] * inv[None, :]
    ang = np.concatenate([ang_r, ang_r, ang_c, ang_c], axis=-1).astype(np.float32)
    cos = np.cos(ang).astype(np.float32)
    sin = np.sin(ang).astype(np.float32)
    sign = np.where((np.arange(HEAD_DIM) % AXIS_DIM) < HALF_AXIS, -1.0, 1.0).astype(np.float32)
    reps = LANES // HEAD_DIM
    return jnp.asarray(np.tile(cos, (1, reps))), jnp.asarray(np.tile(sin * sign, (1, reps)))


def _prepare(norm1_g, w_in, q_norm_g, k_norm_g, conv_dw_w, conv_dw_b, conv_ln_g, conv_ln_b, w_pw2,
             attn_out_g, conv_out_g, w_out, norm2_g, router_w, router_b, w_gate_up, b_gate_up,
             w_down, b_down, final_g):
    row = lambda a: a.reshape(1, -1).astype(F32)
    qk_gain = jnp.concatenate([jnp.tile(q_norm_g[0] * (math.log2(math.e) / math.sqrt(HEAD_DIM)), N_Q_HEADS),
                               jnp.tile(k_norm_g[0], N_KV_HEADS)]).reshape(1, D_QK)
    head = np.arange(LANES) // HEAD_DIM
    hsum = jnp.asarray(np.tile((head[:, None] == head[None, :]).astype(np.float32), (2, 1)), dtype=BF16)
    rw = jnp.pad(router_w[0].astype(F32), ((0, 0), (0, LANES - N_EXPERTS)))
    rw_hi = rw.astype(BF16)
    rw_lo = (rw - rw_hi.astype(F32)).astype(BF16)
    rb = jnp.pad(router_b[0].astype(F32), (0, LANES - N_EXPERTS)).reshape(1, LANES)
    low = np.tril(np.ones((N_EXPERTS, N_EXPERTS), np.float32), k=-1)
    bgu = b_gate_up[0]
    return dict(
        norm1_g=row(norm1_g[0]), w_in=w_in[0].astype(BF16), qk_gain=qk_gain, hsum=hsum,
        dw_w=conv_dw_w[0].astype(F32), dw_b=row(conv_dw_b[0]), ln_g=row(conv_ln_g[0]),
        ln_b=row(conv_ln_b[0]), w_pw2=w_pw2[0].astype(BF16), conv_out_g=row(conv_out_g[0]),
        attn_out_g=row(attn_out_g[0]), w_out=w_out[0].astype(BF16), norm2_g=row(norm2_g[0]),
        rw=jnp.concatenate([rw_hi, rw_lo], axis=1), rb=rb, low=jnp.asarray(low, dtype=BF16),
        w_gu=_wprep(w_gate_up[0]),
        b_gu=jnp.concatenate([bgu[:, 0::2].reshape(N_EXPERTS, -1, FF_CHUNK),
                              bgu[:, 1::2].reshape(N_EXPERTS, -1, FF_CHUNK)],
                             axis=-1).reshape(N_EXPERTS, 1, 2 * D_FF),
        w_dn=w_down[0].astype(BF16), b_dn=b_down[0].reshape(N_EXPERTS, 1, D_MODEL),
        final_g=row(final_g),
    )


def _trunk(x, p):
    B, S, _ = x.shape
    T = B * S
    cos, sin = _rope_tables(S)
    q, k, v, u = _pre(x, p["norm1_g"], p["w_in"], p["qk_gain"], cos, sin, p["hsum"])
    tile = min(ROUTE_ROWS, S)
    tri = jnp.asarray(np.triu(np.ones((tile, tile), np.float32), k=1), dtype=BF16)
    x1, h2, lpos, gates, seg, cnt = _mix(q, k, v, u, x, dict(p, tri=tri))

    n_t = T // tile
    seg = seg[:, :, 0:3].astype(I32)
    seg_n, seg_base, seg_off = seg[:, :, 0], seg[:, :, 1], seg[:, :, 2]
    counts = cnt[:, 0].astype(I32)
    padded = ((counts + EXPERT_ROWS - 1) // EXPERT_ROWS) * EXPERT_ROWS
    pend = jnp.cumsum(padded)
    pstart = pend - padded
    seg_dst = pstart[None, :] + seg_base
    n_blocks = (T * TOP_K + n_t * N_EXPERTS * SEG_ALIGN) // EXPERT_ROWS + N_EXPERTS
    n_active = pend[-1] // EXPERT_ROWS
    blk_start = jnp.minimum(jnp.arange(n_blocks, dtype=I32), n_active - 1) * EXPERT_ROWS
    block_expert = jnp.sum((blk_start[:, None] >= pend[None, :]).astype(I32), axis=1)
    block_expert = jnp.minimum(block_expert, N_EXPERTS - 1).astype(I32)
    flat = lambda a: a.reshape(-1).astype(I32)

    n_active = n_active.reshape(1).astype(I32)
    xs = _dispatch(flat(seg_n), flat(seg_off), flat(seg_dst), flat(pstart + counts),
                   flat(padded - counts), n_active, lpos, h2, n_blocks * EXPERT_ROWS)
    ys = _experts(block_expert, n_active, xs,
                  p["w_gu"], p["b_gu"], p["w_dn"], p["b_dn"])
    y = _combine(flat(seg_n), flat(seg_off), flat(seg_dst), lpos, gates,
                 x1.reshape(T, D_MODEL), p["final_g"], ys)
    return y.reshape(B, S, D_MODEL)


def kernel(x_prompt, x_sample, norm1_g, w_in, q_norm_g, k_norm_g, conv_dw_w, conv_dw_b, conv_ln_g,
           conv_ln_b, w_pw2, attn_out_g, conv_out_g, w_out, norm2_g, router_w, router_b, w_gate_up,
           b_gate_up, w_down, b_down, final_g):
    p = _prepare(norm1_g, w_in, q_norm_g, k_norm_g, conv_dw_w, conv_dw_b, conv_ln_g, conv_ln_b,
                 w_pw2, attn_out_g, conv_out_g, w_out, norm2_g, router_w, router_b, w_gate_up,
                 b_gate_up, w_down, b_down, final_g)
    return (_trunk(x_prompt, p), _trunk(x_sample, p))
```

```python
import functools
import math

import numpy as np
import jax
import jax.numpy as jnp
from jax import lax
from jax.experimental import pallas as pl
from jax.experimental.pallas import tpu as pltpu

F32 = jnp.float32
BF16 = jnp.bfloat16
I32 = jnp.int32
U32 = jnp.uint32

D_MODEL = 1024
GRID_W = 64
HEAD_DIM = 64
N_Q_HEADS = 8
N_KV_HEADS = 2
N_GROUPS_PER_KV = N_Q_HEADS // N_KV_HEADS
D_ATTN = N_Q_HEADS * HEAD_DIM
D_KV = N_KV_HEADS * HEAD_DIM
D_QK = D_ATTN + D_KV
AXIS_DIM = HEAD_DIM // 2
HALF_AXIS = AXIS_DIM // 2
ROPE_THETA = 10000.0
D_CONV = D_MODEL - D_ATTN
CONV_K = 31
CONV_PAD = (CONV_K - 1) // 2
D_IN_PROJ = D_ATTN + 2 * D_KV + 2 * D_CONV
N_EXPERTS = 32
TOP_K = 4
D_FF = D_MODEL
SWIGLU_ALPHA = 1.702
SWIGLU_LIMIT = 7.0
EPS = 1e-6

LANES = 128
SUBLANES = 8
HALO = 16
NEG_INF = float("-inf")
D_HALF = D_MODEL // 2
SEG_ALIGN = SUBLANES

PRE_ROWS = 1024
MIX_ROWS = 512
CONV_ROWS = 256
ROUTE_ROWS = 256
EXPERT_ROWS = 512
WPREP_ROWS = 512
FF_CHUNK = D_FF

VMEM_LIMIT = 56 * 1024 * 1024


def _cparams(n_axes):
    return pltpu.CompilerParams(dimension_semantics=("arbitrary",) * n_axes,
                                vmem_limit_bytes=VMEM_LIMIT)


def _rms(x, g):
    return x * lax.rsqrt(jnp.mean(x * x, axis=-1, keepdims=True) + EPS) * g


def _split_bf16(x):
    hi = x.astype(BF16)
    lo = (x - hi.astype(F32)).astype(BF16)
    return hi, lo


def _pack_rows(x):
    bits = lax.bitcast_convert_type(x, U32)
    return (bits[:, :D_HALF] & jnp.uint32(0xFFFF0000)) | (bits[:, D_HALF:] >> 16)


def _unpack_rows(w):
    left = lax.bitcast_convert_type(w & jnp.uint32(0xFFFF0000), F32).astype(BF16)
    right = lax.bitcast_convert_type(w << 16, F32).astype(BF16)
    return left, right


def _wprep_body(w_ref, perm_ref, o_ref):
    perm = perm_ref[...]
    group = 2 * LANES
    for g in range(2 * D_FF // group):
        blk = w_ref[:, g * group:(g + 1) * group].astype(BF16)
        r = jnp.dot(blk, perm, preferred_element_type=F32)
        c, half = divmod(g * LANES, FF_CHUNK)
        o_ref[:, 2 * c * FF_CHUNK + half:2 * c * FF_CHUNK + half + LANES] = r[:, :LANES].astype(BF16)
        o_ref[:, (2 * c + 1) * FF_CHUNK + half:(2 * c + 1) * FF_CHUNK + half + LANES] = (
            r[:, LANES:].astype(BF16))


def _wprep(w_gu):
    group = 2 * LANES
    i = np.arange(group)[:, None]
    c = np.arange(group)[None, :]
    perm = np.where(c < LANES, i == 2 * c, i == 2 * (c - LANES) + 1).astype(np.float32)
    return pl.pallas_call(
        _wprep_body,
        grid=(N_EXPERTS, D_MODEL // WPREP_ROWS),
        in_specs=[pl.BlockSpec((None, WPREP_ROWS, 2 * D_FF), lambda e, r: (e, r, 0)),
                  pl.BlockSpec((group, group), lambda e, r: (0, 0))],
        out_specs=pl.BlockSpec((None, WPREP_ROWS, 2 * D_FF), lambda e, r: (e, r, 0)),
        out_shape=jax.ShapeDtypeStruct((N_EXPERTS, D_MODEL, 2 * D_FF), BF16),
        compiler_params=_cparams(2),
        name="wprep",
    )(w_gu, jnp.asarray(perm, dtype=BF16))


def _pre_body(x_ref, g1_ref, win_ref, qkg_ref, cos_ref, sin_ref, hsum_ref,
              q_ref, k_ref, v_ref, u_ref):
    h = _rms(x_ref[...], g1_ref[...]).astype(BF16)
    z = jnp.dot(h, win_ref[...], preferred_element_type=F32)
    cos = cos_ref[...]
    sin = sin_ref[...]
    lane = lax.broadcasted_iota(I32, (1, LANES), 1)
    first_half = (lane % AXIS_DIM) < HALF_AXIS
    hsum = hsum_ref[...]
    for c in range(D_QK // LANES):
        zc = z[:, c * LANES:(c + 1) * LANES]
        hi, lo = _split_bf16(zc * zc)
        ss = jnp.dot(jnp.concatenate([hi, lo], axis=1), hsum, preferred_element_type=F32)
        zn = zc * lax.rsqrt(ss * (1.0 / HEAD_DIM) + EPS) * qkg_ref[:, c * LANES:(c + 1) * LANES]
        rot = jnp.where(first_half,
                        pltpu.roll(zn, LANES - HALF_AXIS, 1),
                        pltpu.roll(zn, HALF_AXIS, 1))
        out = (zn * cos + rot * sin).astype(BF16)
        if c < D_ATTN // LANES:
            q_ref[:, c * LANES:(c + 1) * LANES] = out
        else:
            k_ref[...] = out
    zv = z[:, D_QK:D_QK + D_KV]
    ones_col = jnp.where(lane == HEAD_DIM, 1.0, 0.0)
    for kv in range(N_KV_HEADS):
        zk = zv if kv == 0 else pltpu.roll(zv, LANES - kv * HEAD_DIM, 1)
        v_ref[:, kv * LANES:(kv + 1) * LANES] = jnp.where(lane < HEAD_DIM, zk, ones_col).astype(BF16)
    a = z[:, D_QK + D_KV:D_QK + D_KV + D_CONV]
    gate = z[:, D_QK + D_KV + D_CONV:]
    u_ref[...] = a * jax.nn.sigmoid(gate)


def _pre(x, g1, w_in, qk_gain, cos, sin, hsum):
    B, S, _ = x.shape
    rows = min(PRE_ROWS, S)
    grid = (B, S // rows)
    row_blk = lambda w: pl.BlockSpec((None, rows, w), lambda b, i: (b, i, 0))
    full = lambda a: pl.BlockSpec(a.shape, lambda b, i: (0,) * a.ndim)
    return pl.pallas_call(
        _pre_body,
        grid=grid,
        in_specs=[row_blk(D_MODEL), full(g1), full(w_in), full(qk_gain),
                  pl.BlockSpec((rows, LANES), lambda b, i: (i, 0)),
                  pl.BlockSpec((rows, LANES), lambda b, i: (i, 0)),
                  full(hsum)],
        out_specs=[row_blk(D_ATTN), row_blk(D_KV), row_blk(N_KV_HEADS * LANES), row_blk(D_CONV)],
        out_shape=[jax.ShapeDtypeStruct((B, S, D_ATTN), BF16),
                   jax.ShapeDtypeStruct((B, S, D_KV), BF16),
                   jax.ShapeDtypeStruct((B, S, N_KV_HEADS * LANES), BF16),
                   jax.ShapeDtypeStruct((B, S, D_CONV), F32)],
        compiler_params=_cparams(2),
        name="pre",
    )(x, g1, w_in, qk_gain, cos, sin, hsum)


def _attend(q_ref, k_ref, v_ref, attn_ref):
    nt = (((1,), (1,)), ((), ()))
    for kv in range(N_KV_HEADS):
        kh = k_ref[:, kv * HEAD_DIM:(kv + 1) * HEAD_DIM]
        vh = v_ref[:, kv * LANES:(kv + 1) * LANES]
        for g in range(N_GROUPS_PER_KV):
            h = kv * N_GROUPS_PER_KV + g
            qh = q_ref[:, h * HEAD_DIM:(h + 1) * HEAD_DIM]
            s = lax.dot_general(qh, kh, nt, preferred_element_type=F32)
            m = jnp.max(s, axis=-1, keepdims=True)
            p = jnp.exp2(s - m).astype(BF16)
            o = jnp.dot(p, vh, preferred_element_type=F32)
            attn_ref[:, h * HEAD_DIM:(h + 1) * HEAD_DIM] = (
                o[:, :HEAD_DIM] / o[:, HEAD_DIM:HEAD_DIM + 1]).astype(BF16)


def _top4(lg, iota):
    vals, sels = [], []
    for _ in range(TOP_K):
        m = jnp.max(lg, axis=0, keepdims=True)
        idx = jnp.min(jnp.where(lg == m, iota, N_EXPERTS), axis=0, keepdims=True)
        sel = iota == idx
        vals.append(m)
        sels.append(sel)
        lg = jnp.where(sel, NEG_INF, lg)
    return vals, sels


def _route(lg, tri_ref, low_ref, run_ref, lpos_ref, gate_ref, seg_ref, cols):
    tile = lg.shape[0]
    lgt = lg.T[0:N_EXPERTS, :]
    eio = lax.broadcasted_iota(I32, (N_EXPERTS, tile), 0)
    vals, sels = _top4(lgt, eio)
    es = [jnp.exp(v - vals[0]) for v in vals]
    inv = 1.0 / (es[0] + es[1] + es[2] + es[3])
    onehot = jnp.where(sels[0] | sels[1] | sels[2] | sels[3], 1.0, 0.0)
    prefix = jnp.dot(onehot.astype(BF16), tri_ref[...], preferred_element_type=F32)
    n_tok = jnp.sum(onehot, axis=1, keepdims=True)
    n_seg = jnp.maximum(jnp.floor((n_tok + (SEG_ALIGN - 1)) * (1.0 / SEG_ALIGN)), 1.0) * SEG_ALIGN
    off = jnp.dot(low_ref[...], jnp.broadcast_to(n_seg, (N_EXPERTS, LANES)).astype(BF16),
                  preferred_element_type=F32)
    pos = off[:, 0:1] + prefix
    for kk in range(TOP_K):
        gate_ref[kk:kk + 1, cols] = es[kk] * inv
        lpos_ref[kk:kk + 1, cols] = jnp.sum(jnp.where(sels[kk], pos, 0.0), axis=0,
                                            keepdims=True).astype(I32)
    lane = lax.broadcasted_iota(I32, (N_EXPERTS, LANES), 1)
    seg_ref[...] = jnp.where(lane == 0, n_seg, jnp.where(lane == 1, run_ref[...],
                                                         jnp.where(lane == 2, off, 0.0)))
    run_ref[...] = run_ref[...] + n_seg


def _mix_body(q_ref, k_ref, v_ref, up_ref, um_ref, un_ref, x_ref,
              dww_ref, dwb_ref, lng_ref, lnb_ref, wpw_ref, cog_ref, aog_ref, wout_ref,
              g2_ref, rw_ref, rb_ref, tri_ref, low_ref,
              x1_ref, h2_ref, lpos_ref, gate_ref, seg_ref, cnt_ref,
              attn_ref, wpad_ref, shift_ref, conv_ref, run_ref):
    b = pl.program_id(0)
    i = pl.program_id(1)
    n_i = pl.num_programs(1)
    rows = um_ref.shape[0]
    tile = tri_ref.shape[0]

    @pl.when((b == 0) & (i == 0))
    def _():
        run_ref[...] = jnp.zeros_like(run_ref)

    wpad_ref[0:HALO, :] = jnp.where(i > 0, up_ref[...], 0.0)
    wpad_ref[HALO:HALO + rows, :] = um_ref[...]
    wpad_ref[HALO + rows:, :] = jnp.where(i < n_i - 1, un_ref[...], 0.0)
    span = rows + 2 * HALO - SUBLANES
    for r in range(1, SUBLANES):
        shift_ref[r - 1] = wpad_ref[r:r + span, :]
    part = min(CONV_ROWS, rows)
    for c in range(D_CONV // LANES):
        cs = slice(c * LANES, (c + 1) * LANES)
        for r0 in range(0, rows, part):
            acc = jnp.zeros((part, LANES), F32)
            for j in range(CONV_K):
                off = HALO - CONV_PAD + j
                r, base = off % SUBLANES, r0 + off - off % SUBLANES
                tap = (wpad_ref[base:base + part, cs] if r == 0
                       else shift_ref[r - 1, base:base + part, cs])
                acc = acc + tap * dww_ref[j:j + 1, cs]
            conv_ref[r0:r0 + part, cs] = acc + dwb_ref[:, cs]
    _attend(q_ref, k_ref, v_ref, attn_ref)
    y = conv_ref[...]
    mu = jnp.mean(y, axis=-1, keepdims=True)
    yc = y - mu
    var = jnp.mean(yc * yc, axis=-1, keepdims=True)
    y = yc * lax.rsqrt(var + EPS) * lng_ref[...] + lnb_ref[...]
    y = y * jax.nn.sigmoid(y)
    conv_out = jnp.dot(y.astype(BF16), wpw_ref[...], preferred_element_type=F32)

    cn = _rms(conv_out, cog_ref[...]).astype(BF16)
    an = _rms(attn_ref[...].astype(F32), aog_ref[...]).astype(BF16)
    mix = (jnp.dot(cn, wout_ref[0:D_CONV, :], preferred_element_type=F32)
           + jnp.dot(an, wout_ref[D_CONV:, :], preferred_element_type=F32))
    x1 = x_ref[...] + mix
    x1_ref[...] = x1
    h2 = _rms(x1, g2_ref[...])
    hi, lo = _split_bf16(h2)
    h2_ref[...] = hi

    hi_part = jnp.dot(hi, rw_ref[...], preferred_element_type=F32)
    lg = (hi_part[:, :LANES] + hi_part[:, LANES:]
          + jnp.dot(lo, rw_ref[:, 0:LANES], preferred_element_type=F32)) + rb_ref[...]
    for t in range(rows // tile):
        cols = slice(t * tile, (t + 1) * tile)
        _route(lg[cols, :], tri_ref, low_ref, run_ref, lpos_ref, gate_ref, seg_ref.at[t], cols)
    cnt_ref[...] = run_ref[...]


def _mix(q, k, v, u, x, p):
    B, S, _ = x.shape
    rows = min(MIX_ROWS, S)
    tile = p["tri"].shape[0]
    n_i = S // rows
    T = B * S
    hb = rows // HALO
    n_halo = S // HALO
    full = lambda a: pl.BlockSpec(a.shape, lambda b, i: (0,) * a.ndim)
    row_blk = lambda w: pl.BlockSpec((None, rows, w), lambda b, i: (b, i, 0))
    seq_blk = lambda w: pl.BlockSpec((None, S, w), lambda b, i: (b, 0, 0))
    tok_blk = lambda r: pl.BlockSpec((r, rows), lambda b, i: (0, b * n_i + i))
    weights = [p["dw_w"], p["dw_b"], p["ln_g"], p["ln_b"], p["w_pw2"], p["conv_out_g"],
               p["attn_out_g"], p["w_out"], p["norm2_g"], p["rw"], p["rb"],
               p["tri"], p["low"]]
    return pl.pallas_call(
        _mix_body,
        grid=(B, n_i),
        in_specs=[row_blk(D_ATTN), seq_blk(D_KV), seq_blk(N_KV_HEADS * LANES),
                  pl.BlockSpec((None, HALO, D_CONV), lambda b, i: (b, jnp.maximum(i * hb - 1, 0), 0)),
                  row_blk(D_CONV),
                  pl.BlockSpec((None, HALO, D_CONV),
                               lambda b, i: (b, jnp.minimum((i + 1) * hb, n_halo - 1), 0)),
                  row_blk(D_MODEL)] + [full(w) for w in weights],
        out_specs=[row_blk(D_MODEL),
                   pl.BlockSpec((rows, D_MODEL), lambda b, i: (b * n_i + i, 0)),
                   tok_blk(TOP_K), tok_blk(TOP_K),
                   pl.BlockSpec((rows // tile, N_EXPERTS, LANES), lambda b, i: (b * n_i + i, 0, 0)),
                   pl.BlockSpec((N_EXPERTS, LANES), lambda b, i: (0, 0))],
        out_shape=[jax.ShapeDtypeStruct((B, S, D_MODEL), F32),
                   jax.ShapeDtypeStruct((T, D_MODEL), BF16),
                   jax.ShapeDtypeStruct((TOP_K, T), I32),
                   jax.ShapeDtypeStruct((TOP_K, T), F32),
                   jax.ShapeDtypeStruct((T // tile, N_EXPERTS, LANES), F32),
                   jax.ShapeDtypeStruct((N_EXPERTS, LANES), F32)],
        scratch_shapes=[pltpu.VMEM((rows, D_ATTN), BF16),
                        pltpu.VMEM((rows + 2 * HALO, D_CONV), F32),
                        pltpu.VMEM((SUBLANES - 1, rows + 2 * HALO - SUBLANES, D_CONV), F32),
                        pltpu.VMEM((rows, D_CONV), F32),
                        pltpu.VMEM((N_EXPERTS, LANES), F32)],
        compiler_params=_cparams(2),
        name="mix",
    )(q, k, v, u, u, u, x, *weights)


def _segments(n_ref, tile, fn, maybe_empty=False):
    for e in range(N_EXPERTS):
        n = n_ref[tile * N_EXPERTS + e]
        if maybe_empty:
            pl.when(n > 0)(functools.partial(fn, e, pl.multiple_of(n, SEG_ALIGN)))
        else:
            fn(e, pl.multiple_of(n, SEG_ALIGN))


def _tile_rows(n_ref, off_ref, tile):
    last = tile * N_EXPERTS + N_EXPERTS - 1
    return pl.multiple_of(off_ref[last] + n_ref[last], SEG_ALIGN)


def _slot_matrix(lpos_ref, vals, n_slots):
    tile = lpos_ref.shape[1]
    sio = lax.broadcasted_iota(I32, (n_slots, tile), 0)
    m = jnp.zeros((n_slots, tile), F32)
    for kk in range(TOP_K):
        v = 1.0 if vals is None else vals[kk:kk + 1, :]
        m = jnp.where(sio == lpos_ref[kk:kk + 1, :], v, m)
    return m


def _dispatch_body(n_ref, off_ref, dst_ref, tlo_ref, tn_ref, nact_ref,
                   lpos_ref, h2_ref, xs_ref, buf_ref, zero_ref, sem, zsem):
    j = pl.program_id(0)
    n_j = pl.num_programs(0)
    slot = j % 2
    n_slots = buf_ref.shape[1]

    def seg_copy(tile, sl, e, n):
        src = buf_ref.at[sl, pl.ds(pl.multiple_of(off_ref[tile * N_EXPERTS + e], SEG_ALIGN), n), :]
        dst = xs_ref.at[pl.ds(pl.multiple_of(dst_ref[tile * N_EXPERTS + e], SEG_ALIGN), n), :]
        return pltpu.make_async_copy(src, dst, sem.at[sl])

    def tail_copy(e, n):
        return pltpu.make_async_copy(
            zero_ref.at[pl.ds(0, n), :],
            xs_ref.at[pl.ds(pl.multiple_of(tlo_ref[e], SEG_ALIGN), n), :], zsem)

    def unused_block_copy(b):
        rows = zero_ref.shape[0]
        return pltpu.make_async_copy(
            zero_ref, xs_ref.at[pl.ds(pl.multiple_of(b * rows, rows), rows), :], zsem)

    @pl.when(j == 0)
    def _():
        zero_ref[...] = jnp.zeros_like(zero_ref)
        _segments(tn_ref, 0, lambda e, n: tail_copy(e, n).start(), maybe_empty=True)
        _segments(tn_ref, 0, lambda e, n: tail_copy(e, n).wait(), maybe_empty=True)
        n_blocks = xs_ref.shape[0] // zero_ref.shape[0]

        def start(b, c):
            unused_block_copy(b).start()
            return c

        def wait(b, c):
            unused_block_copy(b).wait()
            return c

        lax.fori_loop(nact_ref[0], n_blocks, start, 0)
        lax.fori_loop(nact_ref[0], n_blocks, wait, 0)

    def wait_tile(tile, sl):
        total = _tile_rows(n_ref, off_ref, tile)
        pltpu.make_async_copy(buf_ref.at[sl, pl.ds(0, total), :], xs_ref.at[pl.ds(0, total), :],
                              sem.at[sl]).wait()

    @pl.when(j >= 2)
    def _():
        wait_tile(j - 2, slot)

    sel = _slot_matrix(lpos_ref, None, n_slots).astype(BF16)
    rows = jnp.dot(sel, h2_ref[...], preferred_element_type=F32)
    buf_ref[slot] = _pack_rows(rows)
    _segments(n_ref, j, lambda e, n: seg_copy(j, slot, e, n).start())

    @pl.when(j == n_j - 1)
    def _():
        @pl.when(j >= 1)
        def _():
            wait_tile(j - 1, 1 - slot)
        wait_tile(j, slot)


def _dispatch(seg_n, seg_off, seg_dst, tail_lo, tail_n, n_active, lpos, h2, n_rows):
    tile = min(ROUTE_ROWS, h2.shape[0])
    n_t = h2.shape[0] // tile
    n_slots = TOP_K * tile + N_EXPERTS * SEG_ALIGN
    return pl.pallas_call(
        _dispatch_body,
        grid_spec=pltpu.PrefetchScalarGridSpec(
            num_scalar_prefetch=6,
            grid=(n_t,),
            in_specs=[pl.BlockSpec((TOP_K, tile), lambda j, *_: (0, j)),
                      pl.BlockSpec((tile, D_MODEL), lambda j, *_: (j, 0))],
            out_specs=pl.BlockSpec(memory_space=pl.ANY),
            scratch_shapes=[pltpu.VMEM((2, n_slots, D_HALF), U32),
                            pltpu.VMEM((EXPERT_ROWS, D_HALF), U32),
                            pltpu.SemaphoreType.DMA((2,)), pltpu.SemaphoreType.DMA],
        ),
        out_shape=jax.ShapeDtypeStruct((n_rows, D_HALF), U32),
        compiler_params=_cparams(1),
        name="dispatch",
    )(seg_n, seg_off, seg_dst, tail_lo, tail_n, n_active, lpos, h2)


def _combine_body(n_ref, off_ref, dst_ref, lpos_ref, gate_ref, x1_ref, fg_ref, ys_ref,
                  y_ref, buf_ref, sem):
    j = pl.program_id(0)
    n_j = pl.num_programs(0)
    slot = j % 2
    n_slots = buf_ref.shape[1]

    def seg_copy(tile, sl, e, n):
        src = ys_ref.at[pl.ds(pl.multiple_of(dst_ref[tile * N_EXPERTS + e], SEG_ALIGN), n), :]
        dst = buf_ref.at[sl, pl.ds(pl.multiple_of(off_ref[tile * N_EXPERTS + e], SEG_ALIGN), n), :]
        return pltpu.make_async_copy(src, dst, sem.at[sl])

    @pl.when(j == 0)
    def _():
        buf_ref[...] = jnp.zeros_like(buf_ref)
        _segments(n_ref, 0, lambda e, n: seg_copy(0, 0, e, n).start())

    def wait_tile(tile, sl):
        total = _tile_rows(n_ref, off_ref, tile)
        pltpu.make_async_copy(ys_ref.at[pl.ds(0, total), :], buf_ref.at[sl, pl.ds(0, total), :],
                              sem.at[sl]).wait()

    nxt = jnp.minimum(j + 1, n_j - 1)
    _segments(n_ref, nxt, lambda e, n: seg_copy(nxt, 1 - slot, e, n).start())
    w = _slot_matrix(lpos_ref, gate_ref, n_slots).T.astype(BF16)
    wait_tile(j, slot)
    left, right = _unpack_rows(buf_ref[slot])
    moe = jnp.concatenate(
        [jnp.dot(w, half, preferred_element_type=F32) for half in (left, right)], axis=1)
    y_ref[...] = _rms(x1_ref[...] + moe, fg_ref[...])

    @pl.when(j == n_j - 1)
    def _():
        wait_tile(nxt, 1 - slot)


def _combine(seg_n, seg_off, seg_dst, lpos, gates, x1, final_g, ys):
    T = x1.shape[0]
    tile = min(ROUTE_ROWS, T)
    n_t = T // tile
    n_slots = TOP_K * tile + N_EXPERTS * SEG_ALIGN
    return pl.pallas_call(
        _combine_body,
        grid_spec=pltpu.PrefetchScalarGridSpec(
            num_scalar_prefetch=3,
            grid=(n_t,),
            in_specs=[pl.BlockSpec((TOP_K, tile), lambda j, *_: (0, j)),
                      pl.BlockSpec((TOP_K, tile), lambda j, *_: (0, j)),
                      pl.BlockSpec((tile, D_MODEL), lambda j, *_: (j, 0)),
                      pl.BlockSpec((1, D_MODEL), lambda j, *_: (0, 0)),
                      pl.BlockSpec(memory_space=pl.ANY)],
            out_specs=pl.BlockSpec((tile, D_MODEL), lambda j, *_: (j, 0)),
            scratch_shapes=[pltpu.VMEM((2, n_slots, D_HALF), U32), pltpu.SemaphoreType.DMA((2,))],
        ),
        out_shape=jax.ShapeDtypeStruct((T, D_MODEL), F32),
        compiler_params=_cparams(1),
        name="combine",
    )(seg_n, seg_off, seg_dst, lpos, gates, x1, final_g, ys)


def _expert_body(be_ref, nact_ref, xs_ref, wgu_ref, bgu_ref, wdn_ref, bdn_ref, ys_ref):
    @pl.when(pl.program_id(0) < nact_ref[0])
    def _():
        left, right = _unpack_rows(xs_ref[...])
        hgu = (jnp.dot(left, wgu_ref[0:D_HALF, :], preferred_element_type=F32)
               + jnp.dot(right, wgu_ref[D_HALF:, :], preferred_element_type=F32)) + bgu_ref[...]
        glu = jnp.minimum(hgu[:, :D_FF], SWIGLU_LIMIT)
        lin = jnp.clip(hgu[:, D_FF:], -SWIGLU_LIMIT, SWIGLU_LIMIT)
        act = glu * jax.nn.sigmoid(SWIGLU_ALPHA * glu) * (lin + 1.0)
        y = jnp.dot(act.astype(BF16), wdn_ref[...], preferred_element_type=F32) + bdn_ref[...]
        ys_ref[...] = _pack_rows(y.astype(BF16).astype(F32))


def _experts(block_expert, n_active, xs, w_gu, b_gu, w_dn, b_dn):
    n_rows = xs.shape[0]
    n_blocks = n_rows // EXPERT_ROWS
    blk = lambda i, be, na: (jnp.minimum(i, na[0] - 1), 0)
    per_e = lambda i, be, na: (be[i], 0, 0)
    return pl.pallas_call(
        _expert_body,
        grid_spec=pltpu.PrefetchScalarGridSpec(
            num_scalar_prefetch=2,
            grid=(n_blocks,),
            in_specs=[pl.BlockSpec((EXPERT_ROWS, D_HALF), blk),
                      pl.BlockSpec((None, D_MODEL, 2 * D_FF), per_e),
                      pl.BlockSpec((None, 1, 2 * D_FF), per_e),
                      pl.BlockSpec((None, D_FF, D_MODEL), per_e),
                      pl.BlockSpec((None, 1, D_MODEL), per_e)],
            out_specs=pl.BlockSpec((EXPERT_ROWS, D_HALF), blk),
        ),
        out_shape=jax.ShapeDtypeStruct((n_rows, D_HALF), U32),
        input_output_aliases={2: 0},
        compiler_params=_cparams(1),
        name="experts",
    )(block_expert, n_active, xs, w_gu, b_gu, w_dn, b_dn)


def _rope_tables(seq_len):
    t = np.arange(seq_len)
    inv = 1.0 / (ROPE_THETA ** (np.arange(0, AXIS_DIM, 2, dtype=np.float32) / AXIS_DIM))
    inv = inv.astype(np.float32)
    ang_r = (t // GRID_W).astype(np.float32)[:, None] * inv[None, :]
    ang_c = (t % GRID_W).astype(np.float32)[:, None] * inv[None, :]
    ang = np.concatenate([ang_r, ang_r, ang_c, ang_c], axis=-1).astype(np.float32)
    cos = np.cos(ang).astype(np.float32)
    sin = np.sin(ang).astype(np.float32)
    sign = np.where((np.arange(HEAD_DIM) % AXIS_DIM) < HALF_AXIS, -1.0, 1.0).astype(np.float32)
    reps = LANES // HEAD_DIM
    return jnp.asarray(np.tile(cos, (1, reps))), jnp.asarray(np.tile(sin * sign, (1, reps)))


def _prepare(norm1_g, w_in, q_norm_g, k_norm_g, conv_dw_w, conv_dw_b, conv_ln_g, conv_ln_b, w_pw2,
             attn_out_g, conv_out_g, w_out, norm2_g, router_w, router_b, w_gate_up, b_gate_up,
             w_down, b_down, final_g):
    row = lambda a: a.reshape(1, -1).astype(F32)
    qk_gain = jnp.concatenate([jnp.tile(q_norm_g[0] * (math.log2(math.e) / math.sqrt(HEAD_DIM)), N_Q_HEADS),
                               jnp.tile(k_norm_g[0], N_KV_HEADS)]).reshape(1, D_QK)
    head = np.arange(LANES) // HEAD_DIM
    hsum = jnp.asarray(np.tile((head[:, None] == head[None, :]).astype(np.float32), (2, 1)), dtype=BF16)
    rw = jnp.pad(router_w[0].astype(F32), ((0, 0), (0, LANES - N_EXPERTS)))
    rw_hi = rw.astype(BF16)
    rw_lo = (rw - rw_hi.astype(F32)).astype(BF16)
    rb = jnp.pad(router_b[0].astype(F32), (0, LANES - N_EXPERTS)).reshape(1, LANES)
    low = np.tril(np.ones((N_EXPERTS, N_EXPERTS), np.float32), k=-1)
    bgu = b_gate_up[0]
    return dict(
        norm1_g=row(norm1_g[0]), w_in=w_in[0].astype(BF16), qk_gain=qk_gain, hsum=hsum,
        dw_w=conv_dw_w[0].astype(F32), dw_b=row(conv_dw_b[0]), ln_g=row(conv_ln_g[0]),
        ln_b=row(conv_ln_b[0]), w_pw2=w_pw2[0].astype(BF16), conv_out_g=row(conv_out_g[0]),
        attn_out_g=row(attn_out_g[0]), w_out=w_out[0].astype(BF16), norm2_g=row(norm2_g[0]),
        rw=jnp.concatenate([rw_hi, rw_lo], axis=1), rb=rb, low=jnp.asarray(low, dtype=BF16),
        w_gu=_wprep(w_gate_up[0]),
        b_gu=jnp.concatenate([bgu[:, 0::2].reshape(N_EXPERTS, -1, FF_CHUNK),
                              bgu[:, 1::2].reshape(N_EXPERTS, -1, FF_CHUNK)],
                             axis=-1).reshape(N_EXPERTS, 1, 2 * D_FF),
        w_dn=w_down[0].astype(BF16), b_dn=b_down[0].reshape(N_EXPERTS, 1, D_MODEL),
        final_g=row(final_g),
    )


def _trunk(x, p):
    B, S, _ = x.shape
    T = B * S
    cos, sin = _rope_tables(S)
    q, k, v, u = _pre(x, p["norm1_g"], p["w_in"], p["qk_gain"], cos, sin, p["hsum"])
    tile = min(ROUTE_ROWS, S)
    tri = jnp.asarray(np.triu(np.ones((tile, tile), np.float32), k=1), dtype=BF16)
    x1, h2, lpos, gates, seg, cnt = _mix(q, k, v, u, x, dict(p, tri=tri))

    n_t = T // tile
    seg = seg[:, :, 0:3].astype(I32)
    seg_n, seg_base, seg_off = seg[:, :, 0], seg[:, :, 1], seg[:, :, 2]
    counts = cnt[:, 0].astype(I32)
    padded = ((counts + EXPERT_ROWS - 1) // EXPERT_ROWS) * EXPERT_ROWS
    pend = jnp.cumsum(padded)
    pstart = pend - padded
    seg_dst = pstart[None, :] + seg_base
    n_blocks = (T * TOP_K + n_t * N_EXPERTS * SEG_ALIGN) // EXPERT_ROWS + N_EXPERTS
    n_active = pend[-1] // EXPERT_ROWS
    blk_start = jnp.minimum(jnp.arange(n_blocks, dtype=I32), n_active - 1) * EXPERT_ROWS
    block_expert = jnp.sum((blk_start[:, None] >= pend[None, :]).astype(I32), axis=1)
    block_expert = jnp.minimum(block_expert, N_EXPERTS - 1).astype(I32)
    flat = lambda a: a.reshape(-1).astype(I32)

    n_active = n_active.reshape(1).astype(I32)
    xs = _dispatch(flat(seg_n), flat(seg_off), flat(seg_dst), flat(pstart + counts),
                   flat(padded - counts), n_active, lpos, h2, n_blocks * EXPERT_ROWS)
    ys = _experts(block_expert, n_active, xs,
                  p["w_gu"], p["b_gu"], p["w_dn"], p["b_dn"])
    y = _combine(flat(seg_n), flat(seg_off), flat(seg_dst), lpos, gates,
                 x1.reshape(T, D_MODEL), p["final_g"], ys)
    return y.reshape(B, S, D_MODEL)


def kernel(x_prompt, x_sample, norm1_g, w_in, q_norm_g, k_norm_g, conv_dw_w, conv_dw_b, conv_ln_g,
           conv_ln_b, w_pw2, attn_out_g, conv_out_g, w_out, norm2_g, router_w, router_b, w_gate_up,
           b_gate_up, w_down, b_down, final_g):
    p = _prepare(norm1_g, w_in, q_norm_g, k_norm_g, conv_dw_w, conv_dw_b, conv_ln_g, conv_ln_b,
                 w_pw2, attn_out_g, conv_out_g, w_out, norm2_g, router_w, router_b, w_gate_up,
                 b_gate_up, w_down, b_down, final_g)
    return (_trunk(x_prompt, p), _trunk(x_sample, p))
```

```python
import functools
import math

import numpy as np
import jax
import jax.numpy as jnp
from jax import lax
from jax.experimental import pallas as pl
from jax.experimental.pallas import tpu as pltpu

F32 = jnp.float32
BF16 = jnp.bfloat16
I32 = jnp.int32
U32 = jnp.uint32

D_MODEL = 1024
GRID_W = 64
HEAD_DIM = 64
N_Q_HEADS = 8
N_KV_HEADS = 2
N_GROUPS_PER_KV = N_Q_HEADS // N_KV_HEADS
D_ATTN = N_Q_HEADS * HEAD_DIM
D_KV = N_KV_HEADS * HEAD_DIM
D_QK = D_ATTN + D_KV
AXIS_DIM = HEAD_DIM // 2
HALF_AXIS = AXIS_DIM // 2
ROPE_THETA = 10000.0
D_CONV = D_MODEL - D_ATTN
CONV_K = 31
CONV_PAD = (CONV_K - 1) // 2
D_IN_PROJ = D_ATTN + 2 * D_KV + 2 * D_CONV
N_EXPERTS = 32
TOP_K = 4
D_FF = D_MODEL
SWIGLU_ALPHA = 1.702
SWIGLU_LIMIT = 7.0
EPS = 1e-6

LANES = 128
SUBLANES = 8
HALO = 16
NEG_INF = float("-inf")
D_HALF = D_MODEL // 2
SEG_ALIGN = SUBLANES

PRE_ROWS = 1024
MIX_ROWS = 512
CONV_ROWS = 256
ROUTE_ROWS = 256
EXPERT_ROWS = 512
WPREP_ROWS = 512
FF_CHUNK = D_FF

VMEM_LIMIT = 56 * 1024 * 1024


def _cparams(n_axes):
    return pltpu.CompilerParams(dimension_semantics=("arbitrary",) * n_axes,
                                vmem_limit_bytes=VMEM_LIMIT)


def _rms(x, g):
    return x * lax.rsqrt(jnp.mean(x * x, axis=-1, keepdims=True) + EPS) * g


def _split_bf16(x):
    hi = x.astype(BF16)
    lo = (x - hi.astype(F32)).astype(BF16)
    return hi, lo


def _pack_rows(x):
    bits = lax.bitcast_convert_type(x, U32)
    return (bits[:, :D_HALF] & jnp.uint32(0xFFFF0000)) | (bits[:, D_HALF:] >> 16)


def _unpack_rows(w):
    left = lax.bitcast_convert_type(w & jnp.uint32(0xFFFF0000), F32).astype(BF16)
    right = lax.bitcast_convert_type(w << 16, F32).astype(BF16)
    return left, right


def _wprep_body(w_ref, perm_ref, o_ref):
    perm = perm_ref[...]
    group = 2 * LANES
    for g in range(2 * D_FF // group):
        blk = w_ref[:, g * group:(g + 1) * group].astype(BF16)
        r = jnp.dot(blk, perm, preferred_element_type=F32)
        c, half = divmod(g * LANES, FF_CHUNK)
        o_ref[:, 2 * c * FF_CHUNK + half:2 * c * FF_CHUNK + half + LANES] = r[:, :LANES].astype(BF16)
        o_ref[:, (2 * c + 1) * FF_CHUNK + half:(2 * c + 1) * FF_CHUNK + half + LANES] = (
            r[:, LANES:].astype(BF16))


def _wprep(w_gu):
    group = 2 * LANES
    i = np.arange(group)[:, None]
    c = np.arange(group)[None, :]
    perm = np.where(c < LANES, i == 2 * c, i == 2 * (c - LANES) + 1).astype(np.float32)
    return pl.pallas_call(
        _wprep_body,
        grid=(N_EXPERTS, D_MODEL // WPREP_ROWS),
        in_specs=[pl.BlockSpec((None, WPREP_ROWS, 2 * D_FF), lambda e, r: (e, r, 0)),
                  pl.BlockSpec((group, group), lambda e, r: (0, 0))],
        out_specs=pl.BlockSpec((None, WPREP_ROWS, 2 * D_FF), lambda e, r: (e, r, 0)),
        out_shape=jax.ShapeDtypeStruct((N_EXPERTS, D_MODEL, 2 * D_FF), BF16),
        compiler_params=_cparams(2),
        name="wprep",
    )(w_gu, jnp.asarray(perm, dtype=BF16))


def _pre_body(x_ref, g1_ref, win_ref, qkg_ref, cos_ref, sin_ref, hsum_ref,
              q_ref, k_ref, v_ref, u_ref):
    h = _rms(x_ref[...], g1_ref[...]).astype(BF16)
    z = jnp.dot(h, win_ref[...], preferred_element_type=F32)
    cos = cos_ref[...]
    sin = sin_ref[...]
    lane = lax.broadcasted_iota(I32, (1, LANES), 1)
    first_half = (lane % AXIS_DIM) < HALF_AXIS
    hsum = hsum_ref[...]
    for c in range(D_QK // LANES):
        zc = z[:, c * LANES:(c + 1) * LANES]
        hi, lo = _split_bf16(zc * zc)
        ss = jnp.dot(jnp.concatenate([hi, lo], axis=1), hsum, preferred_element_type=F32)
        zn = zc * lax.rsqrt(ss * (1.0 / HEAD_DIM) + EPS) * qkg_ref[:, c * LANES:(c + 1) * LANES]
        rot = jnp.where(first_half,
                        pltpu.roll(zn, LANES - HALF_AXIS, 1),
                        pltpu.roll(zn, HALF_AXIS, 1))
        out = (zn * cos + rot * sin).astype(BF16)
        if c < D_ATTN // LANES:
            q_ref[:, c * LANES:(c + 1) * LANES] = out
        else:
            k_ref[...] = out
    zv = z[:, D_QK:D_QK + D_KV]
    ones_col = jnp.where(lane == HEAD_DIM, 1.0, 0.0)
    for kv in range(N_KV_HEADS):
        zk = zv if kv == 0 else pltpu.roll(zv, LANES - kv * HEAD_DIM, 1)
        v_ref[:, kv * LANES:(kv + 1) * LANES] = jnp.where(lane < HEAD_DIM, zk, ones_col).astype(BF16)
    a = z[:, D_QK + D_KV:D_QK + D_KV + D_CONV]
    gate = z[:, D_QK + D_KV + D_CONV:]
    u_ref[...] = a * jax.nn.sigmoid(gate)


def _pre(x, g1, w_in, qk_gain, cos, sin, hsum):
    B, S, _ = x.shape
    rows = min(PRE_ROWS, S)
    grid = (B, S // rows)
    row_blk = lambda w: pl.BlockSpec((None, rows, w), lambda b, i: (b, i, 0))
    full = lambda a: pl.BlockSpec(a.shape, lambda b, i: (0,) * a.ndim)
    return pl.pallas_call(
        _pre_body,
        grid=grid,
        in_specs=[row_blk(D_MODEL), full(g1), full(w_in), full(qk_gain),
                  pl.BlockSpec((rows, LANES), lambda b, i: (i, 0)),
                  pl.BlockSpec((rows, LANES), lambda b, i: (i, 0)),
                  full(hsum)],
        out_specs=[row_blk(D_ATTN), row_blk(D_KV), row_blk(N_KV_HEADS * LANES), row_blk(D_CONV)],
        out_shape=[jax.ShapeDtypeStruct((B, S, D_ATTN), BF16),
                   jax.ShapeDtypeStruct((B, S, D_KV), BF16),
                   jax.ShapeDtypeStruct((B, S, N_KV_HEADS * LANES), BF16),
                   jax.ShapeDtypeStruct((B, S, D_CONV), F32)],
        compiler_params=_cparams(2),
        name="pre",
    )(x, g1, w_in, qk_gain, cos, sin, hsum)


def _attend(q_ref, k_ref, v_ref, attn_ref):
    nt = (((1,), (1,)), ((), ()))
    for kv in range(N_KV_HEADS):
        kh = k_ref[:, kv * HEAD_DIM:(kv + 1) * HEAD_DIM]
        vh = v_ref[:, kv * LANES:(kv + 1) * LANES]
        for g in range(N_GROUPS_PER_KV):
            h = kv * N_GROUPS_PER_KV + g
            qh = q_ref[:, h * HEAD_DIM:(h + 1) * HEAD_DIM]
            s = lax.dot_general(qh, kh, nt, preferred_element_type=F32)
            m = jnp.max(s, axis=-1, keepdims=True)
            p = jnp.exp2(s - m).astype(BF16)
            o = jnp.dot(p, vh, preferred_element_type=F32)
            attn_ref[:, h * HEAD_DIM:(h + 1) * HEAD_DIM] = (
                o[:, :HEAD_DIM] / o[:, HEAD_DIM:HEAD_DIM + 1]).astype(BF16)


def _top4(lg, iota):
    vals, sels = [], []
    for _ in range(TOP_K):
        m = jnp.max(lg, axis=0, keepdims=True)
        idx = jnp.min(jnp.where(lg == m, iota, N_EXPERTS), axis=0, keepdims=True)
        sel = iota == idx
        vals.append(m)
        sels.append(sel)
        lg = jnp.where(sel, NEG_INF, lg)
    return vals, sels


def _route(lg, tri_ref, low_ref, run_ref, lpos_ref, gate_ref, seg_ref, cols):
    tile = lg.shape[0]
    lgt = lg.T[0:N_EXPERTS, :]
    eio = lax.broadcasted_iota(I32, (N_EXPERTS, tile), 0)
    vals, sels = _top4(lgt, eio)
    es = [jnp.exp(v - vals[0]) for v in vals]
    inv = 1.0 / (es[0] + es[1] + es[2] + es[3])
    onehot = jnp.where(sels[0] | sels[1] | sels[2] | sels[3], 1.0, 0.0)
    prefix = jnp.dot(onehot.astype(BF16), tri_ref[...], preferred_element_type=F32)
    n_tok = jnp.sum(onehot, axis=1, keepdims=True)
    n_seg = jnp.maximum(jnp.floor((n_tok + (SEG_ALIGN - 1)) * (1.0 / SEG_ALIGN)), 1.0) * SEG_ALIGN
    off = jnp.dot(low_ref[...], jnp.broadcast_to(n_seg, (N_EXPERTS, LANES)).astype(BF16),
                  preferred_element_type=F32)
    pos = off[:, 0:1] + prefix
    for kk in range(TOP_K):
        gate_ref[kk:kk + 1, cols] = es[kk] * inv
        lpos_ref[kk:kk + 1, cols] = jnp.sum(jnp.where(sels[kk], pos, 0.0), axis=0,
                                            keepdims=True).astype(I32)
    lane = lax.broadcasted_iota(I32, (N_EXPERTS, LANES), 1)
    seg_ref[...] = jnp.where(lane == 0, n_seg, jnp.where(lane == 1, run_ref[...],
                                                         jnp.where(lane == 2, off, 0.0)))
    run_ref[...] = run_ref[...] + n_seg


def _mix_body(q_ref, k_ref, v_ref, up_ref, um_ref, un_ref, x_ref,
              dww_ref, dwb_ref, lng_ref, lnb_ref, wpw_ref, cog_ref, aog_ref, wout_ref,
              g2_ref, rw_ref, rb_ref, tri_ref, low_ref,
              x1_ref, h2_ref, lpos_ref, gate_ref, seg_ref, cnt_ref,
              attn_ref, wpad_ref, shift_ref, conv_ref, run_ref):
    b = pl.program_id(0)
    i = pl.program_id(1)
    n_i = pl.num_programs(1)
    rows = um_ref.shape[0]
    tile = tri_ref.shape[0]

    @pl.when((b == 0) & (i == 0))
    def _():
        run_ref[...] = jnp.zeros_like(run_ref)

    wpad_ref[0:HALO, :] = jnp.where(i > 0, up_ref[...], 0.0)
    wpad_ref[HALO:HALO + rows, :] = um_ref[...]
    wpad_ref[HALO + rows:, :] = jnp.where(i < n_i - 1, un_ref[...], 0.0)
    span = rows + 2 * HALO - SUBLANES
    for r in range(1, SUBLANES):
        shift_ref[r - 1] = wpad_ref[r:r + span, :]
    part = min(CONV_ROWS, rows)
    for c in range(D_CONV // LANES):
        cs = slice(c * LANES, (c + 1) * LANES)
        for r0 in range(0, rows, part):
            acc = jnp.zeros((part, LANES), F32)
            for j in range(CONV_K):
                off = HALO - CONV_PAD + j
                r, base = off % SUBLANES, r0 + off - off % SUBLANES
                tap = (wpad_ref[base:base + part, cs] if r == 0
                       else shift_ref[r - 1, base:base + part, cs])
                acc = acc + tap * dww_ref[j:j + 1, cs]
            conv_ref[r0:r0 + part, cs] = acc + dwb_ref[:, cs]
    _attend(q_ref, k_ref, v_ref, attn_ref)
    y = conv_ref[...]
    mu = jnp.mean(y, axis=-1, keepdims=True)
    yc = y - mu
    var = jnp.mean(yc * yc, axis=-1, keepdims=True)
    y = yc * lax.rsqrt(var + EPS) * lng_ref[...] + lnb_ref[...]
    y = y * jax.nn.sigmoid(y)
    conv_out = jnp.dot(y.astype(BF16), wpw_ref[...], preferred_element_type=F32)

    cn = _rms(conv_out, cog_ref[...]).astype(BF16)
    an = _rms(attn_ref[...].astype(F32), aog_ref[...]).astype(BF16)
    mix = (jnp.dot(cn, wout_ref[0:D_CONV, :], preferred_element_type=F32)
           + jnp.dot(an, wout_ref[D_CONV:, :], preferred_element_type=F32))
    x1 = x_ref[...] + mix
    x1_ref[...] = x1
    h2 = _rms(x1, g2_ref[...])
    hi, lo = _split_bf16(h2)
    h2_ref[...] = hi

    hi_part = jnp.dot(hi, rw_ref[...], preferred_element_type=F32)
    lg = (hi_part[:, :LANES] + hi_part[:, LANES:]
          + jnp.dot(lo, rw_ref[:, 0:LANES], preferred_element_type=F32)) + rb_ref[...]
    for t in range(rows // tile):
        cols = slice(t * tile, (t + 1) * tile)
        _route(lg[cols, :], tri_ref, low_ref, run_ref, lpos_ref, gate_ref, seg_ref.at[t], cols)
    cnt_ref[...] = run_ref[...]


def _mix(q, k, v, u, x, p):
    B, S, _ = x.shape
    rows = min(MIX_ROWS, S)
    tile = p["tri"].shape[0]
    n_i = S // rows
    T = B * S
    hb = rows // HALO
    n_halo = S // HALO
    full = lambda a: pl.BlockSpec(a.shape, lambda b, i: (0,) * a.ndim)
    row_blk = lambda w: pl.BlockSpec((None, rows, w), lambda b, i: (b, i, 0))
    seq_blk = lambda w: pl.BlockSpec((None, S, w), lambda b, i: (b, 0, 0))
    tok_blk = lambda r: pl.BlockSpec((r, rows), lambda b, i: (0, b * n_i + i))
    weights = [p["dw_w"], p["dw_b"], p["ln_g"], p["ln_b"], p["w_pw2"], p["conv_out_g"],
               p["attn_out_g"], p["w_out"], p["norm2_g"], p["rw"], p["rb"],
               p["tri"], p["low"]]
    return pl.pallas_call(
        _mix_body,
        grid=(B, n_i),
        in_specs=[row_blk(D_ATTN), seq_blk(D_KV), seq_blk(N_KV_HEADS * LANES),
                  pl.BlockSpec((None, HALO, D_CONV), lambda b, i: (b, jnp.maximum(i * hb - 1, 0), 0)),
                  row_blk(D_CONV),
                  pl.BlockSpec((None, HALO, D_CONV),
                               lambda b, i: (b, jnp.minimum((i + 1) * hb, n_halo - 1), 0)),
                  row_blk(D_MODEL)] + [full(w) for w in weights],
        out_specs=[row_blk(D_MODEL),
                   pl.BlockSpec((rows, D_MODEL), lambda b, i: (b * n_i + i, 0)),
                   tok_blk(TOP_K), tok_blk(TOP_K),
                   pl.BlockSpec((rows // tile, N_EXPERTS, LANES), lambda b, i: (b * n_i + i, 0, 0)),
                   pl.BlockSpec((N_EXPERTS, LANES), lambda b, i: (0, 0))],
        out_shape=[jax.ShapeDtypeStruct((B, S, D_MODEL), F32),
                   jax.ShapeDtypeStruct((T, D_MODEL), BF16),
                   jax.ShapeDtypeStruct((TOP_K, T), I32),
                   jax.ShapeDtypeStruct((TOP_K, T), F32),
                   jax.ShapeDtypeStruct((T // tile, N_EXPERTS, LANES), F32),
                   jax.ShapeDtypeStruct((N_EXPERTS, LANES), F32)],
        scratch_shapes=[pltpu.VMEM((rows, D_ATTN), BF16),
                        pltpu.VMEM((rows + 2 * HALO, D_CONV), F32),
                        pltpu.VMEM((SUBLANES - 1, rows + 2 * HALO - SUBLANES, D_CONV), F32),
                        pltpu.VMEM((rows, D_CONV), F32),
                        pltpu.VMEM((N_EXPERTS, LANES), F32)],
        compiler_params=_cparams(2),
        name="mix",
    )(q, k, v, u, u, u, x, *weights)


def _segments(n_ref, tile, fn, maybe_empty=False):
    for e in range(N_EXPERTS):
        n = n_ref[tile * N_EXPERTS + e]
        if maybe_empty:
            pl.when(n > 0)(functools.partial(fn, e, pl.multiple_of(n, SEG_ALIGN)))
        else:
            fn(e, pl.multiple_of(n, SEG_ALIGN))


def _tile_rows(n_ref, off_ref, tile):
    last = tile * N_EXPERTS + N_EXPERTS - 1
    return pl.multiple_of(off_ref[last] + n_ref[last], SEG_ALIGN)


def _slot_matrix(lpos_ref, vals, n_slots):
    tile = lpos_ref.shape[1]
    sio = lax.broadcasted_iota(I32, (n_slots, tile), 0)
    m = jnp.zeros((n_slots, tile), F32)
    for kk in range(TOP_K):
        v = 1.0 if vals is None else vals[kk:kk + 1, :]
        m = jnp.where(sio == lpos_ref[kk:kk + 1, :], v, m)
    return m


def _dispatch_body(n_ref, off_ref, dst_ref, tlo_ref, tn_ref, nact_ref,
                   lpos_ref, h2_ref, xs_ref, buf_ref, zero_ref, sem, zsem):
    j = pl.program_id(0)
    n_j = pl.num_programs(0)
    slot = j % 2
    n_slots = buf_ref.shape[1]

    def seg_copy(tile, sl, e, n):
        src = buf_ref.at[sl, pl.ds(pl.multiple_of(off_ref[tile * N_EXPERTS + e], SEG_ALIGN), n), :]
        dst = xs_ref.at[pl.ds(pl.multiple_of(dst_ref[tile * N_EXPERTS + e], SEG_ALIGN), n), :]
        return pltpu.make_async_copy(src, dst, sem.at[sl])

    def tail_copy(e, n):
        return pltpu.make_async_copy(
            zero_ref.at[pl.ds(0, n), :],
            xs_ref.at[pl.ds(pl.multiple_of(tlo_ref[e], SEG_ALIGN), n), :], zsem)

    def unused_block_copy(b):
        rows = zero_ref.shape[0]
        return pltpu.make_async_copy(
            zero_ref, xs_ref.at[pl.ds(pl.multiple_of(b * rows, rows), rows), :], zsem)

    @pl.when(j == 0)
    def _():
        zero_ref[...] = jnp.zeros_like(zero_ref)
        _segments(tn_ref, 0, lambda e, n: tail_copy(e, n).start(), maybe_empty=True)
        _segments(tn_ref, 0, lambda e, n: tail_copy(e, n).wait(), maybe_empty=True)
        n_blocks = xs_ref.shape[0] // zero_ref.shape[0]

        def start(b, c):
            unused_block_copy(b).start()
            return c

        def wait(b, c):
            unused_block_copy(b).wait()
            return c

        lax.fori_loop(nact_ref[0], n_blocks, start, 0)
        lax.fori_loop(nact_ref[0], n_blocks, wait, 0)

    def wait_tile(tile, sl):
        total = _tile_rows(n_ref, off_ref, tile)
        pltpu.make_async_copy(buf_ref.at[sl, pl.ds(0, total), :], xs_ref.at[pl.ds(0, total), :],
                              sem.at[sl]).wait()

    @pl.when(j >= 2)
    def _():
        wait_tile(j - 2, slot)

    sel = _slot_matrix(lpos_ref, None, n_slots).astype(BF16)
    rows = jnp.dot(sel, h2_ref[...], preferred_element_type=F32)
    buf_ref[slot] = _pack_rows(rows)
    _segments(n_ref, j, lambda e, n: seg_copy(j, slot, e, n).start())

    @pl.when(j == n_j - 1)
    def _():
        @pl.when(j >= 1)
        def _():
            wait_tile(j - 1, 1 - slot)
        wait_tile(j, slot)


def _dispatch(seg_n, seg_off, seg_dst, tail_lo, tail_n, n_active, lpos, h2, n_rows):
    tile = min(ROUTE_ROWS, h2.shape[0])
    n_t = h2.shape[0] // tile
    n_slots = TOP_K * tile + N_EXPERTS * SEG_ALIGN
    return pl.pallas_call(
        _dispatch_body,
        grid_spec=pltpu.PrefetchScalarGridSpec(
            num_scalar_prefetch=6,
            grid=(n_t,),
            in_specs=[pl.BlockSpec((TOP_K, tile), lambda j, *_: (0, j)),
                      pl.BlockSpec((tile, D_MODEL), lambda j, *_: (j, 0))],
            out_specs=pl.BlockSpec(memory_space=pl.ANY),
            scratch_shapes=[pltpu.VMEM((2, n_slots, D_HALF), U32),
                            pltpu.VMEM((EXPERT_ROWS, D_HALF), U32),
                            pltpu.SemaphoreType.DMA((2,)), pltpu.SemaphoreType.DMA],
        ),
        out_shape=jax.ShapeDtypeStruct((n_rows, D_HALF), U32),
        compiler_params=_cparams(1),
        name="dispatch",
    )(seg_n, seg_off, seg_dst, tail_lo, tail_n, n_active, lpos, h2)


def _combine_body(n_ref, off_ref, dst_ref, lpos_ref, gate_ref, x1_ref, fg_ref, ys_ref,
                  y_ref, buf_ref, sem):
    j = pl.program_id(0)
    n_j = pl.num_programs(0)
    slot = j % 2
    n_slots = buf_ref.shape[1]

    def seg_copy(tile, sl, e, n):
        src = ys_ref.at[pl.ds(pl.multiple_of(dst_ref[tile * N_EXPERTS + e], SEG_ALIGN), n), :]
        dst = buf_ref.at[sl, pl.ds(pl.multiple_of(off_ref[tile * N_EXPERTS + e], SEG_ALIGN), n), :]
        return pltpu.make_async_copy(src, dst, sem.at[sl])

    @pl.when(j == 0)
    def _():
        buf_ref[...] = jnp.zeros_like(buf_ref)
        _segments(n_ref, 0, lambda e, n: seg_copy(0, 0, e, n).start())

    def wait_tile(tile, sl):
        total = _tile_rows(n_ref, off_ref, tile)
        pltpu.make_async_copy(ys_ref.at[pl.ds(0, total), :], buf_ref.at[sl, pl.ds(0, total), :],
                              sem.at[sl]).wait()

    nxt = jnp.minimum(j + 1, n_j - 1)
    _segments(n_ref, nxt, lambda e, n: seg_copy(nxt, 1 - slot, e, n).start())
    w = _slot_matrix(lpos_ref, gate_ref, n_slots).T.astype(BF16)
    wait_tile(j, slot)
    left, right = _unpack_rows(buf_ref[slot])
    moe = jnp.concatenate(
        [jnp.dot(w, half, preferred_element_type=F32) for half in (left, right)], axis=1)
    y_ref[...] = _rms(x1_ref[...] + moe, fg_ref[...])

    @pl.when(j == n_j - 1)
    def _():
        wait_tile(nxt, 1 - slot)


def _combine(seg_n, seg_off, seg_dst, lpos, gates, x1, final_g, ys):
    T = x1.shape[0]
    tile = min(ROUTE_ROWS, T)
    n_t = T // tile
    n_slots = TOP_K * tile + N_EXPERTS * SEG_ALIGN
    return pl.pallas_call(
        _combine_body,
        grid_spec=pltpu.PrefetchScalarGridSpec(
            num_scalar_prefetch=3,
            grid=(n_t,),
            in_specs=[pl.BlockSpec((TOP_K, tile), lambda j, *_: (0, j)),
                      pl.BlockSpec((TOP_K, tile), lambda j, *_: (0, j)),
                      pl.BlockSpec((tile, D_MODEL), lambda j, *_: (j, 0)),
                      pl.BlockSpec((1, D_MODEL), lambda j, *_: (0, 0)),
                      pl.BlockSpec(memory_space=pl.ANY)],
            out_specs=pl.BlockSpec((tile, D_MODEL), lambda j, *_: (j, 0)),
            scratch_shapes=[pltpu.VMEM((2, n_slots, D_HALF), U32), pltpu.SemaphoreType.DMA((2,))],
        ),
        out_shape=jax.ShapeDtypeStruct((T, D_MODEL), F32),
        compiler_params=_cparams(1),
        name="combine",
    )(seg_n, seg_off, seg_dst, lpos, gates, x1, final_g, ys)


def _expert_body(be_ref, nact_ref, xs_ref, wgu_ref, bgu_ref, wdn_ref, bdn_ref, ys_ref):
    @pl.when(pl.program_id(0) < nact_ref[0])
    def _():
        left, right = _unpack_rows(xs_ref[...])
        hgu = (jnp.dot(left, wgu_ref[0:D_HALF, :], preferred_element_type=F32)
               + jnp.dot(right, wgu_ref[D_HALF:, :], preferred_element_type=F32)) + bgu_ref[...]
        glu = jnp.minimum(hgu[:, :D_FF], SWIGLU_LIMIT)
        lin = jnp.clip(hgu[:, D_FF:], -SWIGLU_LIMIT, SWIGLU_LIMIT)
        act = glu * jax.nn.sigmoid(SWIGLU_ALPHA * glu) * (lin + 1.0)
        y = jnp.dot(act.astype(BF16), wdn_ref[...].astype(BF16),
                    preferred_element_type=F32) + bdn_ref[...]
        ys_ref[...] = _pack_rows(y.astype(BF16).astype(F32))


def _experts(block_expert, n_active, xs, w_gu, b_gu, w_dn, b_dn):
    n_rows = xs.shape[0]
    n_blocks = n_rows // EXPERT_ROWS
    blk = lambda i, be, na: (jnp.minimum(i, na[0] - 1), 0)
    per_e = lambda i, be, na: (be[i], 0, 0)
    return pl.pallas_call(
        _expert_body,
        grid_spec=pltpu.PrefetchScalarGridSpec(
            num_scalar_prefetch=2,
            grid=(n_blocks,),
            in_specs=[pl.BlockSpec((EXPERT_ROWS, D_HALF), blk),
                      pl.BlockSpec((None, D_MODEL, 2 * D_FF), per_e),
                      pl.BlockSpec((None, 1, 2 * D_FF), per_e),
                      pl.BlockSpec((None, D_FF, D_MODEL), per_e),
                      pl.BlockSpec((None, 1, D_MODEL), per_e)],
            out_specs=pl.BlockSpec((EXPERT_ROWS, D_HALF), blk),
        ),
        out_shape=jax.ShapeDtypeStruct((n_rows, D_HALF), U32),
        input_output_aliases={2: 0},
        compiler_params=_cparams(1),
        name="experts",
    )(block_expert, n_active, xs, w_gu, b_gu, w_dn, b_dn)


def _rope_tables(seq_len):
    t = np.arange(seq_len)
    inv = 1.0 / (ROPE_THETA ** (np.arange(0, AXIS_DIM, 2, dtype=np.float32) / AXIS_DIM))
    inv = inv.astype(np.float32)
    ang_r = (t // GRID_W).astype(np.float32)[:, None] * inv[None, :]
    ang_c = (t % GRID_W).astype(np.float32)[:, None] * inv[None, :]
    ang = np.concatenate([ang_r, ang_r, ang_c, ang_c], axis=-1).astype(np.float32)
    cos = np.cos(ang).astype(np.float32)
    sin = np.sin(ang).astype(np.float32)
    sign = np.where((np.arange(HEAD_DIM) % AXIS_DIM) < HALF_AXIS, -1.0, 1.0).astype(np.float32)
    reps = LANES // HEAD_DIM
    return jnp.asarray(np.tile(cos, (1, reps))), jnp.asarray(np.tile(sin * sign, (1, reps)))


def _prepare(norm1_g, w_in, q_norm_g, k_norm_g, conv_dw_w, conv_dw_b, conv_ln_g, conv_ln_b, w_pw2,
             attn_out_g, conv_out_g, w_out, norm2_g, router_w, router_b, w_gate_up, b_gate_up,
             w_down, b_down, final_g):
    row = lambda a: a.reshape(1, -1).astype(F32)
    qk_gain = jnp.concatenate([jnp.tile(q_norm_g[0] * (math.log2(math.e) / math.sqrt(HEAD_DIM)), N_Q_HEADS),
                               jnp.tile(k_norm_g[0], N_KV_HEADS)]).reshape(1, D_QK)
    head = np.arange(LANES) // HEAD_DIM
    hsum = jnp.asarray(np.tile((head[:, None] == head[None, :]).astype(np.float32), (2, 1)), dtype=BF16)
    rw = jnp.pad(router_w[0].astype(F32), ((0, 0), (0, LANES - N_EXPERTS)))
    rw_hi = rw.astype(BF16)
    rw_lo = (rw - rw_hi.astype(F32)).astype(BF16)
    rb = jnp.pad(router_b[0].astype(F32), (0, LANES - N_EXPERTS)).reshape(1, LANES)
    low = np.tril(np.ones((N_EXPERTS, N_EXPERTS), np.float32), k=-1)
    bgu = b_gate_up[0]
    return dict(
        norm1_g=row(norm1_g[0]), w_in=w_in[0].astype(BF16), qk_gain=qk_gain, hsum=hsum,
        dw_w=conv_dw_w[0].astype(F32), dw_b=row(conv_dw_b[0]), ln_g=row(conv_ln_g[0]),
        ln_b=row(conv_ln_b[0]), w_pw2=w_pw2[0].astype(BF16), conv_out_g=row(conv_out_g[0]),
        attn_out_g=row(attn_out_g[0]), w_out=w_out[0].astype(BF16), norm2_g=row(norm2_g[0]),
        rw=jnp.concatenate([rw_hi, rw_lo], axis=1), rb=rb, low=jnp.asarray(low, dtype=BF16),
        w_gu=_wprep(w_gate_up[0]),
        b_gu=jnp.concatenate([bgu[:, 0::2].reshape(N_EXPERTS, -1, FF_CHUNK),
                              bgu[:, 1::2].reshape(N_EXPERTS, -1, FF_CHUNK)],
                             axis=-1).reshape(N_EXPERTS, 1, 2 * D_FF),
        w_dn=w_down[0].astype(F32), b_dn=b_down[0].reshape(N_EXPERTS, 1, D_MODEL),
        final_g=row(final_g),
    )


def _trunk(x, p):
    B, S, _ = x.shape
    T = B * S
    cos, sin = _rope_tables(S)
    q, k, v, u = _pre(x, p["norm1_g"], p["w_in"], p["qk_gain"], cos, sin, p["hsum"])
    tile = min(ROUTE_ROWS, S)
    tri = jnp.asarray(np.triu(np.ones((tile, tile), np.float32), k=1), dtype=BF16)
    x1, h2, lpos, gates, seg, cnt = _mix(q, k, v, u, x, dict(p, tri=tri))

    n_t = T // tile
    seg = seg[:, :, 0:3].astype(I32)
    seg_n, seg_base, seg_off = seg[:, :, 0], seg[:, :, 1], seg[:, :, 2]
    counts = cnt[:, 0].astype(I32)
    padded = ((counts + EXPERT_ROWS - 1) // EXPERT_ROWS) * EXPERT_ROWS
    pend = jnp.cumsum(padded)
    pstart = pend - padded
    seg_dst = pstart[None, :] + seg_base
    n_blocks = (T * TOP_K + n_t * N_EXPERTS * SEG_ALIGN) // EXPERT_ROWS + N_EXPERTS
    n_active = pend[-1] // EXPERT_ROWS
    blk_start = jnp.minimum(jnp.arange(n_blocks, dtype=I32), n_active - 1) * EXPERT_ROWS
    block_expert = jnp.sum((blk_start[:, None] >= pend[None, :]).astype(I32), axis=1)
    block_expert = jnp.minimum(block_expert, N_EXPERTS - 1).astype(I32)
    flat = lambda a: a.reshape(-1).astype(I32)

    n_active = n_active.reshape(1).astype(I32)
    xs = _dispatch(flat(seg_n), flat(seg_off), flat(seg_dst), flat(pstart + counts),
                   flat(padded - counts), n_active, lpos, h2, n_blocks * EXPERT_ROWS)
    ys = _experts(block_expert, n_active, xs,
                  p["w_gu"], p["b_gu"], p["w_dn"], p["b_dn"])
    y = _combine(flat(seg_n), flat(seg_off), flat(seg_dst), lpos, gates,
                 x1.reshape(T, D_MODEL), p["final_g"], ys)
    return y.reshape(B, S, D_MODEL)


def kernel(x_prompt, x_sample, norm1_g, w_in, q_norm_g, k_norm_g, conv_dw_w, conv_dw_b, conv_ln_g,
           conv_ln_b, w_pw2, attn_out_g, conv_out_g, w_out, norm2_g, router_w, router_b, w_gate_up,
           b_gate_up, w_down, b_down, final_g):
    p = _prepare(norm1_g, w_in, q_norm_g, k_norm_g, conv_dw_w, conv_dw_b, conv_ln_g, conv_ln_b,
                 w_pw2, attn_out_g, conv_out_g, w_out, norm2_g, router_w, router_b, w_gate_up,
                 b_gate_up, w_down, b_down, final_g)
    return (_trunk(x_prompt, p), _trunk(x_sample, p))
```
